```python
import math
import jax
import jax.numpy as jnp
from jax import lax
import numpy as np

D_MODEL = 1024
BATCH = 8
SEQ = 4096
DEPTH = 2
DEC_BATCH = 8
DEC_SEQ = 8192
PAST_LEN = 128

D_MIX = D_MODEL
A_WIDTH = D_MIX // 4
A_GROUP_CH = 16
A_GROUPS = A_WIDTH // A_GROUP_CH
A_STATE = 64
B_WIDTH = (D_MIX - A_WIDTH) // 2
HY_ORDER = 2
HY_SHORT = 3
HY_BANDS = 16
HY_POS_FEAT = 1 + 2 * HY_BANDS
HY_FILT_HID = 64
C_WIDTH = D_MIX - A_WIDTH - B_WIDTH
C_HEAD_DIM = 64
C_HEADS = C_WIDTH // C_HEAD_DIM
C_CHUNK = 64
EPS = 1e-6
F_FLOOR = 1e-30

kernel_name = "hybrid_s5_hyena_hgrn2_encoder"


def _rmsnorm(x, w):
    xf = x.astype(jnp.float32)
    return xf * lax.rsqrt(jnp.mean(xf * xf, axis=-1, keepdims=True) + EPS) * w.astype(jnp.float32)


def _split_columns(proj):
    sizes = (A_WIDTH, A_WIDTH, (HY_ORDER + 1) * B_WIDTH, B_WIDTH) + (C_WIDTH,) * 5
    offsets = []
    acc = 0
    for s in sizes[:-1]:
        acc += s
        offsets.append(acc)
    return jnp.split(proj, offsets, axis=-1)


def _s5_combine(e1, e2):
    a1, b1 = e1
    a2, b2 = e2
    return a1 * a2, a2 * b1 + b2


def _s5_mixer(u, lam_re, lam_im, log_dt, b_re, b_im, c_re, c_im, d, glu_w, glu_b):
    f32 = jnp.float32
    bsz, L, _ = u.shape
    ug = u.reshape(bsz, L, A_GROUPS, A_GROUP_CH).astype(jnp.complex64)
    lam = lax.complex(lam_re.astype(f32), lam_im.astype(f32))
    dt = jnp.exp(log_dt.astype(f32))[..., None]
    lam_bar = jnp.exp(lam * dt)
    b_mat = lax.complex(b_re.astype(f32), b_im.astype(f32))
    b_bar = ((lam_bar - 1.0) / lam)[..., None] * b_mat[None]
    c_mat = lax.complex(c_re.astype(f32), c_im.astype(f32))
    y = u * d.astype(f32)
    for direction in range(2):
        bu = jnp.einsum("gph,blgh->blgp", b_bar[direction], ug)
        a = jnp.broadcast_to(lam_bar[direction][None, None], bu.shape)
        _, states = lax.associative_scan(_s5_combine, (a, bu), reverse=(direction == 1), axis=1)
        y = y + jnp.real(jnp.einsum("ghp,blgp->blgh", c_mat[direction], states)).reshape(bsz, L, A_WIDTH)
    z = jax.nn.gelu(y)
    return z * jax.nn.sigmoid(z @ glu_w.astype(f32) + glu_b.astype(f32))


def _short_conv(u, w, b):
    L = u.shape[1]
    pad = HY_SHORT // 2
    up = jnp.pad(u, ((0, 0), (pad, HY_SHORT - 1 - pad), (0, 0)))
    w = w.astype(jnp.float32)
    out = b.astype(jnp.float32)
    for j in range(HY_SHORT):
        out = out + up[:, j:j + L] * w[j]
    return out


def _hyena_filters(L, w1, b1, w2, b2, w3, sin_freq, log_decay):
    f32 = jnp.float32
    pos = jnp.arange(L, dtype=f32)
    t = pos / max(L - 1, 1)
    w = 2.0 * math.pi * pos / L
    bands = jnp.linspace(1e-4, HY_BANDS - 1, HY_BANDS, dtype=f32)
    feats = jnp.concatenate([t[:, None], jnp.cos(w[:, None] * bands), -jnp.sin(w[:, None] * bands)], axis=-1)
    sin_freq = sin_freq.astype(f32)
    h = jnp.sin(sin_freq[0] * (feats @ w1.astype(f32) + b1.astype(f32)))
    h = jnp.sin(sin_freq[1] * (h @ w2.astype(f32) + b2.astype(f32)))
    h = h @ w3.astype(f32)
    window = jnp.exp(-t[:, None] * jnp.exp(log_decay.astype(f32)))
    h = (h * window).reshape(L, 2, HY_ORDER, B_WIDTH)
    two_sided = jnp.concatenate(
        [h[:, 0], jnp.zeros((1, HY_ORDER, B_WIDTH), f32), h[:0:-1, 1]], axis=0)
    two_sided = two_sided * lax.rsqrt(jnp.sum(two_sided * two_sided, axis=0, keepdims=True) + EPS)
    return jnp.fft.rfft(two_sided, axis=0)


def _hyena_mixer(u, conv_w, conv_b, w1, b1, w2, b2, w3, sin_freq, log_decay, skip):
    bsz, L, _ = u.shape
    u = _short_conv(u, conv_w, conv_b)
    parts = jnp.split(u, HY_ORDER + 1, axis=-1)
    gates, z = parts[:-1], parts[-1]
    kf = _hyena_filters(L, w1, b1, w2, b2, w3, sin_freq, log_decay)
    skip = skip.astype(jnp.float32)
    for order in range(HY_ORDER):
        zf = jnp.fft.rfft(z, n=2 * L, axis=1)
        conv = jnp.fft.irfft(zf * kf[None, :, order], n=2 * L, axis=1)[:, :L]
        z = gates[order] * (conv + z * skip[order])
    return z


def _gla_chunk_step(state, chunk):
    q, log_f, k, v = chunk
    cum = jnp.cumsum(log_f, axis=2)
    inter = jnp.einsum("bhtk,bhkv->bhtv", q * jnp.exp(cum), state)
    diff = cum[:, :, :, None, :] - cum[:, :, None, :, :]
    idx = jnp.arange(C_CHUNK)
    lower = (idx[:, None] >= idx[None, :])[None, None, :, :, None]
    decay = jnp.where(lower, jnp.exp(jnp.minimum(diff, 0.0)), 0.0)
    scores = jnp.einsum("bhtk,bhtsk,bhsk->bhts", q, decay, k)
    intra = jnp.einsum("bhts,bhsv->bhtv", scores, v)
    last = cum[:, :, -1:, :]
    new_state = jnp.exp(last[:, :, 0, :])[..., None] * state + jnp.einsum("bhsk,bhsv->bhkv", k * jnp.exp(last - cum), v)
    return new_state, inter + intra


def _gla_scan(q, log_f, k, v):
    bsz, L, H, dk = q.shape
    dv = v.shape[-1]
    n = L // C_CHUNK

    def to_chunks(t):
        return t.reshape(bsz, n, C_CHUNK, H, t.shape[-1]).transpose(1, 0, 3, 2, 4)

    init = jnp.zeros((bsz, H, dk, dv), jnp.float32)
    _, out = lax.scan(_gla_chunk_step, init, (to_chunks(q), to_chunks(log_f), to_chunks(k), to_chunks(v)))
    return out.transpose(1, 0, 3, 2, 4).reshape(bsz, L, H, dv)


def _hgrn2_mixer(q, zf_fwd, zf_bwd, v_in, lb, norm_w):
    bsz, L, _ = q.shape

    def heads(t):
        return t.reshape(bsz, L, C_HEADS, C_HEAD_DIM)

    qh, vh = heads(q), heads(v_in)
    outs = []
    for direction, zf in enumerate((zf_fwd, zf_bwd)):
        lbd = lb[direction]
        f = lbd + (1.0 - lbd) * jax.nn.sigmoid(zf)
        log_f = jnp.log(jnp.maximum(f, F_FLOOR))
        k = (1.0 - lbd) * jax.nn.sigmoid(-zf)
        args = [qh, heads(log_f), heads(k), vh]
        if direction == 1:
            args = [jnp.flip(t, axis=1) for t in args]
        o = _gla_scan(*args)
        if direction == 1:
            o = jnp.flip(o, axis=1)
        outs.append(o)
    o = outs[0] + outs[1]
    o = o * lax.rsqrt(jnp.mean(o * o, axis=-1, keepdims=True) + EPS) * norm_w.astype(jnp.float32)
    return o.reshape(bsz, L, C_WIDTH)


def _trunk(x, c, p):
    f32 = jnp.float32
    lb_soft = jax.nn.softmax(p["hg_lb_logits"].astype(f32), axis=0)
    lower_bounds = jnp.cumsum(lb_soft, axis=0) - lb_soft[0:1]
    cond = jax.nn.silu(c.astype(f32))
    for layer in range(DEPTH):
        ada = cond @ p["ada_w"][layer].astype(f32) + p["ada_b"][layer].astype(f32)
        shift, scale, gate = jnp.split(ada[:, None, :], 3, axis=-1)
        h = _rmsnorm(x, p["norm_w"][layer]) * (1.0 + scale) + shift
        proj = h @ p["w_in"][layer].astype(f32)
        a_u, a_g, b_u, b_g, c_q, c_zf, c_zb, c_v, c_g = _split_columns(proj)
        a_out = _s5_mixer(a_u, p["s5_lambda_re"][layer], p["s5_lambda_im"][layer], p["s5_log_dt"][layer],
                          p["s5_b_re"][layer], p["s5_b_im"][layer], p["s5_c_re"][layer], p["s5_c_im"][layer],
                          p["s5_d"][layer], p["s5_glu_w"][layer], p["s5_glu_b"][layer]) * jax.nn.silu(a_g)
        b_out = _hyena_mixer(b_u, p["hy_conv_w"][layer], p["hy_conv_b"][layer], p["hy_w1"][layer], p["hy_b1"][layer],
                             p["hy_w2"][layer], p["hy_b2"][layer], p["hy_w3"][layer], p["hy_sin_freq"][layer],
                             p["hy_log_decay"][layer], p["hy_skip"][layer]) * jax.nn.silu(b_g)
        c_out = _hgrn2_mixer(c_q, c_zf, c_zb, c_v, lower_bounds[layer], p["hg_norm_w"][layer]) * jax.nn.silu(c_g)
        mixed = jnp.concatenate([a_out, b_out, c_out], axis=-1) @ p["w_out"][layer].astype(f32)
        x = x + (gate * mixed).astype(x.dtype)
    return _rmsnorm(x, p["final_norm_w"]).astype(x.dtype)


def setup_inputs(seed: int = 0) -> dict:
    key = jax.random.key(seed)
    ks = jax.random.split(key, 32)
    f32 = jnp.float32
    D = D_MODEL
    d_in = 2 * A_WIDTH + (HY_ORDER + 2) * B_WIDTH + 5 * C_WIDTH

    def nrm(k, shape, s):
        return s * jax.random.normal(k, shape, f32)

    state_idx = jnp.arange(A_STATE, dtype=f32)
    return {
        "x_prompt": nrm(ks[0], (BATCH, SEQ, D), 1.0),
        "x_sample": nrm(ks[1], (DEC_BATCH, DEC_SEQ, D), 1.0),
        "c_prompt": nrm(ks[2], (BATCH, D), 1.0),
        "c_sample": nrm(ks[3], (DEC_BATCH, D), 1.0),
        "norm_w": 1.0 + nrm(ks[4], (DEPTH, D), 0.02),
        "ada_w": nrm(ks[5], (DEPTH, D, 3 * D), 0.5 * D ** -0.5),
        "ada_b": nrm(ks[6], (DEPTH, 3 * D), 0.01),
        "w_in": nrm(ks[7], (DEPTH, D, d_in), D ** -0.5),
        "w_out": nrm(ks[8], (DEPTH, D_MIX, D), D_MIX ** -0.5),
        "s5_lambda_re": -0.5 + nrm(ks[9], (DEPTH, 2, A_GROUPS, A_STATE), 0.01),
        "s5_lambda_im": math.pi * state_idx + nrm(ks[10], (DEPTH, 2, A_GROUPS, A_STATE), 0.01),
        "s5_log_dt": jax.random.uniform(ks[11], (DEPTH, 2, A_GROUPS), f32, math.log(1e-3), math.log(1e-1)),
        "s5_b_re": nrm(ks[12], (DEPTH, A_GROUPS, A_STATE, A_GROUP_CH), (2 * A_GROUP_CH) ** -0.5),
        "s5_b_im": nrm(ks[13], (DEPTH, A_GROUPS, A_STATE, A_GROUP_CH), (2 * A_GROUP_CH) ** -0.5),
        "s5_c_re": nrm(ks[14], (DEPTH, 2, A_GROUPS, A_GROUP_CH, A_STATE), (2 * A_STATE) ** -0.5),
        "s5_c_im": nrm(ks[15], (DEPTH, 2, A_GROUPS, A_GROUP_CH, A_STATE), (2 * A_STATE) ** -0.5),
        "s5_d": nrm(ks[16], (DEPTH, A_WIDTH), 1.0),
        "s5_glu_w": nrm(ks[17], (DEPTH, A_WIDTH, A_WIDTH), A_WIDTH ** -0.5),
        "s5_glu_b": nrm(ks[18], (DEPTH, A_WIDTH), 0.01),
        "hy_conv_w": nrm(ks[19], (DEPTH, HY_SHORT, (HY_ORDER + 1) * B_WIDTH), HY_SHORT ** -0.5),
        "hy_conv_b": nrm(ks[20], (DEPTH, (HY_ORDER + 1) * B_WIDTH), 0.01),
        "hy_w1": nrm(ks[21], (DEPTH, HY_POS_FEAT, HY_FILT_HID), HY_POS_FEAT ** -0.5),
        "hy_b1": nrm(ks[22], (DEPTH, HY_FILT_HID), 0.1),
        "hy_w2": nrm(ks[23], (DEPTH, HY_FILT_HID, HY_FILT_HID), HY_FILT_HID ** -0.5),
        "hy_b2": nrm(ks[24], (DEPTH, HY_FILT_HID), 0.1),
        "hy_w3": nrm(ks[25], (DEPTH, HY_FILT_HID, 2 * HY_ORDER * B_WIDTH), HY_FILT_HID ** -0.5),
        "hy_sin_freq": 1.0 + nrm(ks[26], (DEPTH, 2, HY_FILT_HID), 0.05),
        "hy_log_decay": jnp.log(jax.random.uniform(ks[27], (DEPTH, 2 * HY_ORDER * B_WIDTH), f32, 3.0, 15.0)),
        "hy_skip": nrm(ks[28], (DEPTH, HY_ORDER, B_WIDTH), 0.5),
        "hg_lb_logits": 1.0 + nrm(ks[29], (DEPTH, 2, C_WIDTH), 0.1),
        "hg_norm_w": 1.0 + nrm(ks[30], (DEPTH, C_HEAD_DIM), 0.02),
        "final_norm_w": 1.0 + nrm(ks[31], (D,), 0.02),
    }


def reference(x_prompt, x_sample, c_prompt, c_sample, norm_w, ada_w, ada_b, w_in, w_out,
              s5_lambda_re, s5_lambda_im, s5_log_dt, s5_b_re, s5_b_im, s5_c_re, s5_c_im, s5_d,
              s5_glu_w, s5_glu_b, hy_conv_w, hy_conv_b, hy_w1, hy_b1, hy_w2, hy_b2, hy_w3,
              hy_sin_freq, hy_log_decay, hy_skip, hg_lb_logits, hg_norm_w, final_norm_w):
    params = dict(
        norm_w=norm_w, ada_w=ada_w, ada_b=ada_b, w_in=w_in, w_out=w_out,
        s5_lambda_re=s5_lambda_re, s5_lambda_im=s5_lambda_im, s5_log_dt=s5_log_dt,
        s5_b_re=s5_b_re, s5_b_im=s5_b_im, s5_c_re=s5_c_re, s5_c_im=s5_c_im, s5_d=s5_d,
        s5_glu_w=s5_glu_w, s5_glu_b=s5_glu_b, hy_conv_w=hy_conv_w, hy_conv_b=hy_conv_b,
        hy_w1=hy_w1, hy_b1=hy_b1, hy_w2=hy_w2, hy_b2=hy_b2, hy_w3=hy_w3, hy_sin_freq=hy_sin_freq,
        hy_log_decay=hy_log_decay, hy_skip=hy_skip, hg_lb_logits=hg_lb_logits, hg_norm_w=hg_norm_w,
        final_norm_w=final_norm_w)
    y_prompt = _trunk(x_prompt, c_prompt, params)
    y_sample = _trunk(x_sample, c_sample, params)
    return (y_prompt, y_sample)
```

```python
import functools
import math

import numpy as np
import jax
import jax.numpy as jnp
from jax import lax
from jax.experimental import pallas as pl
from jax.experimental.pallas import tpu as pltpu

F32 = jnp.float32
BF16 = jnp.bfloat16
HIGHEST = lax.Precision.HIGHEST

D_MODEL = 1024
DEPTH = 2
A_WIDTH = 256
A_GROUP_CH = 16
A_GROUPS = 16
A_STATE = 64
B_WIDTH = 384
HY_ORDER = 2
HY_BANDS = 16
HY_POS_FEAT = 1 + 2 * HY_BANDS
HY_FILT_HID = 64
C_WIDTH = 384
C_HEAD_DIM = 64
EPS = 1e-6
F_FLOOR = 1e-30

LANES = 128
S5_CHUNK = 16
GLA_CHUNK = 64
VMEM_LIMIT = 56 * 1024 * 1024


def _dot(a, b, precision=None):
    return jnp.dot(a, b, preferred_element_type=F32, precision=precision)


def _dot_nt(a, b, precision=None):
    return lax.dot_general(a, b, (((1,), (1,)), ((), ())), preferred_element_type=F32, precision=precision)


def _dot_tn(a, b, precision=None):
    return lax.dot_general(a, b, (((0,), (0,)), ((), ())), preferred_element_type=F32, precision=precision)


def _sigmoid(x):
    return 1.0 / (1.0 + jnp.exp(-x))


def _silu(x):
    return x * _sigmoid(x)


def _params(*sem):
    return pltpu.CompilerParams(dimension_semantics=sem, vmem_limit_bytes=VMEM_LIMIT)


def _ada_kernel(c_ref, w_ref, b_ref, o_ref):
    cond = _silu(c_ref[...])
    o_ref[0] = _dot(cond, w_ref[0], precision=HIGHEST) + b_ref[0]


def _ada(c_all, ada_w, ada_b):
    nb = c_all.shape[0]
    d = D_MODEL
    return pl.pallas_call(
        _ada_kernel,
        grid=(DEPTH, 3),
        in_specs=[pl.BlockSpec((nb, d), lambda l, j: (0, 0)),
                  pl.BlockSpec((1, d, d), lambda l, j: (l, 0, j)),
                  pl.BlockSpec((1, 1, d), lambda l, j: (l, 0, j))],
        out_specs=pl.BlockSpec((1, nb, d), lambda l, j: (l, 0, j)),
        out_shape=jax.ShapeDtypeStruct((DEPTH, nb, 3 * d), F32),
        compiler_params=_params("arbitrary", "arbitrary"),
        name="ada",
    )(c_all, ada_w, ada_b.reshape(DEPTH, 1, 3 * d))


def _proj_kernel(x_ref, mod_ref, nw_ref, wau_ref, wc_ref, wg_ref, wbt_ref, au_ref, c_ref, g_ref, bt_ref):
    x = x_ref[0]
    ms = jnp.mean(x * x, axis=-1, keepdims=True)
    h = x * lax.rsqrt(ms + EPS) * nw_ref[...]
    h = h * (1.0 + mod_ref[0, 1:2, :]) + mod_ref[0, 0:1, :]
    hb = h.astype(BF16)
    au_ref[0] = _dot(hb, wau_ref[...])
    c_ref[0] = _dot(hb, wc_ref[...])
    g_ref[0] = _dot(hb, wg_ref[...])
    bt = _dot_nt(wbt_ref[...], hb)
    for j in range(bt.shape[1] // LANES):
        bt_ref[0, j] = bt[:, j * LANES:(j + 1) * LANES]


def _proj(x, mod, norm_w, wau, wc, wg, wbt, tb):
    bsz, L, d = x.shape
    nbu = wbt.shape[0]
    grid = (bsz, L // tb)
    const = lambda b, i: (0, 0)
    return pl.pallas_call(
        _proj_kernel,
        grid=grid,
        in_specs=[pl.BlockSpec((1, tb, d), lambda b, i: (b, i, 0)),
                  pl.BlockSpec((1, 3, d), lambda b, i: (b, 0, 0)),
                  pl.BlockSpec((1, d), const),
                  pl.BlockSpec(wau.shape, const),
                  pl.BlockSpec(wc.shape, const),
                  pl.BlockSpec(wg.shape, const),
                  pl.BlockSpec(wbt.shape, const)],
        out_specs=[pl.BlockSpec((1, tb, wau.shape[1]), lambda b, i: (b, i, 0)),
                   pl.BlockSpec((1, tb, wc.shape[1]), lambda b, i: (b, i, 0)),
                   pl.BlockSpec((1, tb, wg.shape[1]), lambda b, i: (b, i, 0)),
                   pl.BlockSpec((1, tb // LANES, nbu, LANES), lambda b, i: (b, i, 0, 0))],
        out_shape=[jax.ShapeDtypeStruct((bsz, L, wau.shape[1]), F32),
                   jax.ShapeDtypeStruct((bsz, L, wc.shape[1]), F32),
                   jax.ShapeDtypeStruct((bsz, L, wg.shape[1]), F32),
                   jax.ShapeDtypeStruct((bsz, L // LANES, nbu, LANES), F32)],
        compiler_params=_params("arbitrary", "arbitrary"),
        name="proj",
    )(x, mod, norm_w, wau, wc, wg, wbt)


def _s5_consts(lam_re, lam_im, log_dt, b_re, b_im, c_re, c_im):
    T = S5_CHUNK
    G, P, H = A_GROUPS, A_STATE, A_GROUP_CH
    dt = jnp.exp(log_dt.astype(F32))[..., None]
    lr, li = lam_re.astype(F32) * dt, lam_im.astype(F32) * dt

    def power(k):
        mag = jnp.exp(lr * k)
        return mag * jnp.cos(li * k), mag * jnp.sin(li * k)

    def cmul(ar, ai, br, bi):
        return ar * br - ai * bi, ar * bi + ai * br

    p1r, p1i = power(1.0)
    den = lam_re.astype(F32) ** 2 + lam_im.astype(F32) ** 2
    cfr, cfi = cmul(p1r - 1.0, p1i, lam_re.astype(F32) / den, -lam_im.astype(F32) / den)
    bbr, bbi = cmul(cfr[..., None], cfi[..., None], b_re.astype(F32)[None], b_im.astype(F32)[None])
    cr, ci = c_re.astype(F32), c_im.astype(F32)

    k = jnp.arange(T + 1, dtype=F32)[:, None, None, None]
    pwr, pwi = power(k)
    cpr, cpi = cmul(cr[None], ci[None], pwr[:T, :, :, None, :], pwi[:T, :, :, None, :])
    kern = (jnp.einsum("kdghp,dgpi->dkghi", cpr, bbr, precision=HIGHEST)
            - jnp.einsum("kdghp,dgpi->dkghi", cpi, bbi, precision=HIGHEST))
    s_idx = jnp.arange(T)[:, None]
    t_idx = jnp.arange(T)[None, :]
    kf = jnp.take(kern[0], jnp.clip(t_idx - s_idx, 0, T - 1), axis=0)
    kb = jnp.take(kern[1], jnp.clip(s_idx - t_idx, 0, T - 1), axis=0)
    tmat = (jnp.where((t_idx >= s_idx)[..., None, None, None], kf, 0.0)
            + jnp.where((s_idx >= t_idx)[..., None, None, None], kb, 0.0))
    tmat = tmat.transpose(2, 0, 4, 1, 3).reshape(G, T * H, T * H)

    sf_r, sf_i = cmul(pwr[:T][::-1, 0, :, :, None], pwi[:T][::-1, 0, :, :, None], bbr[0][None], bbi[0][None])
    sb_r, sb_i = cmul(pwr[:T, 1, :, :, None], pwi[:T, 1, :, :, None], bbr[1][None], bbi[1][None])
    bmat = jnp.concatenate([sf_r, sb_r, sf_i, sb_i], axis=2)
    bmat = bmat.transpose(1, 0, 3, 2).reshape(G, T * H, 4 * P)

    mf_r, mf_i = cmul(cr[0][None], ci[0][None], pwr[1:, 0, :, None, :], pwi[1:, 0, :, None, :])
    mb_r, mb_i = cmul(cr[1][None], ci[1][None], pwr[1:][::-1, 1, :, None, :], pwi[1:][::-1, 1, :, None, :])
    zero = jnp.zeros_like(mf_r)
    cf = jnp.concatenate([mf_r, zero, -mf_i, zero], axis=3)
    cb = jnp.concatenate([zero, mb_r, zero, -mb_i], axis=3)
    cf = cf.transpose(1, 3, 0, 2).reshape(G, 4 * P, T * H)
    cb = cb.transpose(1, 3, 0, 2).reshape(G, 4 * P, T * H)

    lam_t = jnp.stack([jnp.concatenate([pwr[T, 0], pwr[T, 1]], axis=-1),
                       jnp.concatenate([pwi[T, 0], pwi[T, 1]], axis=-1)], axis=1)
    return tmat.astype(BF16), bmat.astype(BF16), cf.astype(BF16), cb.astype(BF16), lam_t


def _s5_kernel(u_ref, t_ref, b_ref, cf_ref, cb_ref, lam_ref, y_ref, s_scr, xf_scr, xb_scr):
    u = u_ref[0].astype(BF16)
    s_scr[...] = _dot(u, b_ref[0])
    nchunk = u.shape[0] // 8
    lam_r = jnp.broadcast_to(lam_ref[0, 0:1, :], (8, LANES))
    lam_i = jnp.broadcast_to(lam_ref[0, 1:2, :], (8, LANES))
    fwd_lane = lax.broadcasted_iota(jnp.int32, (8, LANES), 1) < A_STATE

    def step(i, carry):
        xr, xi = carry
        rf = pl.multiple_of(i * 8, 8)
        rb = pl.multiple_of((nchunk - 1 - i) * 8, 8)
        xf_scr[pl.ds(rf, 8), 0:LANES] = xr
        xf_scr[pl.ds(rf, 8), LANES:2 * LANES] = xi
        xb_scr[pl.ds(rb, 8), 0:LANES] = xr
        xb_scr[pl.ds(rb, 8), LANES:2 * LANES] = xi
        sr = jnp.where(fwd_lane, s_scr[pl.ds(rf, 8), 0:LANES], s_scr[pl.ds(rb, 8), 0:LANES])
        si = jnp.where(fwd_lane, s_scr[pl.ds(rf, 8), LANES:2 * LANES], s_scr[pl.ds(rb, 8), LANES:2 * LANES])
        return lam_r * xr - lam_i * xi + sr, lam_r * xi + lam_i * xr + si

    zero = jnp.zeros((8, LANES), F32)
    lax.fori_loop(0, nchunk, step, (zero, zero))
    y_ref[0] = (_dot(u, t_ref[0]) + _dot(xf_scr[...].astype(BF16), cf_ref[0])
                + _dot(xb_scr[...].astype(BF16), cb_ref[0]))


def _s5(uc, consts):
    tmat, bmat, cf, cb, lam_t = consts
    G, R, W = uc.shape
    per_g = lambda g: (g, 0, 0)
    return pl.pallas_call(
        _s5_kernel,
        grid=(G,),
        in_specs=[pl.BlockSpec((1, R, W), per_g),
                  pl.BlockSpec((1, W, W), per_g),
                  pl.BlockSpec((1, W, W), per_g),
                  pl.BlockSpec((1, W, W), per_g),
                  pl.BlockSpec((1, W, W), per_g),
                  pl.BlockSpec((1, 2, LANES), per_g)],
        out_specs=pl.BlockSpec((1, R, W), per_g),
        out_shape=jax.ShapeDtypeStruct((G, R, W), F32),
        scratch_shapes=[pltpu.VMEM((R, W), F32), pltpu.VMEM((R, W), F32), pltpu.VMEM((R, W), F32)],
        compiler_params=_params("arbitrary"),
        name="s5",
    )(uc, tmat, bmat, cf, cb, lam_t)


def _to_chunk_rows(a_u):
    bsz, L, _ = a_u.shape
    t = a_u.reshape(bsz, L // S5_CHUNK, S5_CHUNK, A_GROUPS, A_GROUP_CH)
    return t.transpose(3, 1, 0, 2, 4).reshape(A_GROUPS, (L // S5_CHUNK) * bsz, S5_CHUNK * A_GROUP_CH)


def _from_chunk_rows(y, bsz, L):
    t = y.reshape(A_GROUPS, L // S5_CHUNK, bsz, S5_CHUNK, A_GROUP_CH)
    return t.transpose(2, 1, 3, 0, 4).reshape(bsz, L, A_WIDTH)


def _dft_tables(L):
    N = 2 * L
    n2 = LANES
    n1 = N // n2
    h = n1 // 2
    k = np.arange(n1)
    f1 = np.exp(-2j * np.pi * np.outer(k, k) / n1)
    f2 = np.exp(-2j * np.pi * np.outer(np.arange(n2), np.arange(n2)) / n2)
    tw = np.exp(-2j * np.pi * np.outer(k, np.arange(n2)) / N)
    f1m = np.block([[f1.real[:, :h], -f1.imag[:, :h]], [f1.imag[:, :h], f1.real[:, :h]]])
    ffm = np.concatenate([f1.real, f1.imag], axis=0)
    f2m = np.block([[f2.real, f2.imag], [-f2.imag, f2.real]])
    g2m = np.block([[f2.real, -f2.imag], [f2.imag, f2.real]])
    g1 = np.conj(f1)[:h, :]
    g1m = np.block([[g1.real, -g1.imag], [g1.imag, g1.real]])
    return dict(n1=n1, f1m=f1m, ffm=ffm, f2m=f2m, g2m=g2m, g1m=g1m, twr=tw.real, twi=tw.imag)


def _filter_positions(L):
    N = 2 * L
    i = np.arange(N)
    pos = np.where(i < L, i, N - i).astype(np.float32)
    t = pos / np.float32(max(L - 1, 1))
    w = np.float32(2.0 * math.pi) * pos / np.float32(L)
    bands = np.linspace(1e-4, HY_BANDS - 1, HY_BANDS, dtype=np.float32)
    wb = (w[:, None] * bands).astype(np.float32)
    feats = np.concatenate([t[:, None], np.cos(wb), -np.sin(wb)], axis=-1).astype(np.float32)
    fpad = np.zeros((N, LANES), np.float32)
    fpad[:, :HY_POS_FEAT] = feats
    keep = np.ones((N,), np.float32)
    keep[L] = 0.0
    n1 = N // LANES
    return fpad, t.reshape(n1, 1, LANES), keep.reshape(n1, 1, LANES)


def _filt_kernel(feat_ref, w1_ref, b1_ref, w2_ref, b2_ref, w3t_ref, sf_ref, ld_ref, t_ref, keep_ref, ts_ref, ss_ref):
    f = feat_ref[...]
    h1 = jnp.sin(sf_ref[0:1, :] * (_dot(f, w1_ref[...], precision=HIGHEST) + b1_ref[...]))
    h2 = jnp.sin(sf_ref[1:2, :] * (_dot(h1, w2_ref[...], precision=HIGHEST) + b2_ref[...]))
    h3t = _dot_nt(w3t_ref[0], h2, precision=HIGHEST)
    win = jnp.exp(-jnp.exp(ld_ref[0]) * t_ref[0])
    ts = h3t * win * keep_ref[0]
    ts_ref[0] = ts

    @pl.when(pl.program_id(0) == 0)
    def _():
        ss_ref[...] = jnp.zeros_like(ss_ref)

    ss_ref[...] += ts * ts


def _filt_norm_kernel(ts_ref, ss_ref, o_ref, *, scale):
    tot = jnp.sum(ss_ref[...], axis=-1, keepdims=True)
    o_ref[0] = ts_ref[0] * (lax.rsqrt(tot + EPS) * scale)


def _filt_fft_kernel(ts_ref, ffm_ref, twr_ref, twi_ref, f2m_ref, hr_ref, hi_ref, *, n1, cb):
    a = _dot(ffm_ref[...], ts_ref[...], precision=HIGHEST)
    tr, ti = twr_ref[...], twi_ref[...]
    for c in range(cb):
        sl = slice(c * LANES, (c + 1) * LANES)
        ar, ai = a[:n1, sl], a[n1:, sl]
        cat = jnp.concatenate([ar * tr - ai * ti, ar * ti + ai * tr], axis=1)
        x = _dot(cat, f2m_ref[...], precision=HIGHEST)
        hr_ref[:, sl] = x[:, :LANES]
        hi_ref[:, sl] = x[:, LANES:]


def _hyena_filters(L, w1, b1, w2, b2, w3, sin_freq, log_decay, cb):
    tabs = _dft_tables(L)
    n1 = tabs["n1"]
    N = 2 * L
    nch = HY_ORDER * B_WIDTH
    fpad, tpos, keep = _filter_positions(L)
    w1p = jnp.zeros((LANES, HY_FILT_HID), F32).at[:HY_POS_FEAT].set(w1.astype(F32))
    w3t = w3.astype(F32).T.reshape(2, nch, HY_FILT_HID)
    ld = log_decay.astype(F32).reshape(2, nch, 1)
    half = n1 // 2
    ts, ss = pl.pallas_call(
        _filt_kernel,
        grid=(n1,),
        in_specs=[pl.BlockSpec((LANES, LANES), lambda i: (i, 0)),
                  pl.BlockSpec((LANES, HY_FILT_HID), lambda i: (0, 0)),
                  pl.BlockSpec((1, HY_FILT_HID), lambda i: (0, 0)),
                  pl.BlockSpec((HY_FILT_HID, HY_FILT_HID), lambda i: (0, 0)),
                  pl.BlockSpec((1, HY_FILT_HID), lambda i: (0, 0)),
                  pl.BlockSpec((1, nch, HY_FILT_HID), lambda i: (i // half, 0, 0)),
                  pl.BlockSpec((2, HY_FILT_HID), lambda i: (0, 0)),
                  pl.BlockSpec((1, nch, 1), lambda i: (i // half, 0, 0)),
                  pl.BlockSpec((1, 1, LANES), lambda i: (i, 0, 0)),
                  pl.BlockSpec((1, 1, LANES), lambda i: (i, 0, 0))],
        out_specs=[pl.BlockSpec((1, nch, LANES), lambda i: (i, 0, 0)),
                   pl.BlockSpec((nch, LANES), lambda i: (0, 0))],
        out_shape=[jax.ShapeDtypeStruct((n1, nch, LANES), F32),
                   jax.ShapeDtypeStruct((nch, LANES), F32)],
        compiler_params=_params("arbitrary"),
        name="hy_filter",
    )(jnp.asarray(fpad), w1p, b1.astype(F32).reshape(1, -1), w2.astype(F32), b2.astype(F32).reshape(1, -1),
      w3t, sin_freq.astype(F32), ld, jnp.asarray(tpos), jnp.asarray(keep))
    tsn = pl.pallas_call(
        functools.partial(_filt_norm_kernel, scale=1.0 / N),
        grid=(n1,),
        in_specs=[pl.BlockSpec((1, nch, LANES), lambda i: (i, 0, 0)),
                  pl.BlockSpec((nch, LANES), lambda i: (0, 0))],
        out_specs=pl.BlockSpec((1, nch, LANES), lambda i: (i, 0, 0)),
        out_shape=jax.ShapeDtypeStruct((n1, nch, LANES), F32),
        compiler_params=_params("arbitrary"),
        name="hy_filter_norm",
    )(ts, ss)
    W = cb * LANES
    hr, hi = pl.pallas_call(
        functools.partial(_filt_fft_kernel, n1=n1, cb=cb),
        grid=(nch // cb,),
        in_specs=[pl.BlockSpec((n1, W), lambda j: (0, j)),
                  pl.BlockSpec((2 * n1, n1), lambda j: (0, 0)),
                  pl.BlockSpec((n1, LANES), lambda j: (0, 0)),
                  pl.BlockSpec((n1, LANES), lambda j: (0, 0)),
                  pl.BlockSpec((2 * LANES, 2 * LANES), lambda j: (0, 0))],
        out_specs=[pl.BlockSpec((n1, W), lambda j: (0, j)),
                   pl.BlockSpec((n1, W), lambda j: (0, j))],
        out_shape=[jax.ShapeDtypeStruct((n1, nch * LANES), F32),
                   jax.ShapeDtypeStruct((n1, nch * LANES), F32)],
        compiler_params=_params("arbitrary"),
        name="hy_filter_fft",
    )(tsn.reshape(n1, nch * LANES), jnp.asarray(tabs["ffm"], F32), jnp.asarray(tabs["twr"], F32),
      jnp.asarray(tabs["twi"], F32), jnp.asarray(tabs["f2m"], F32))
    return hr, hi


def _shift_time(x, direction):
    rows = x.shape[0]
    lane = lax.broadcasted_iota(jnp.int32, x.shape, 1)
    row = lax.broadcasted_iota(jnp.int32, x.shape, 0)
    if direction > 0:
        r = pltpu.roll(x, 1, axis=1)
        edge = jnp.where(row == 0, 0.0, pltpu.roll(r, 1, axis=0))
        return jnp.where(lane == 0, edge, r)
    r = pltpu.roll(x, LANES - 1, axis=1)
    edge = jnp.where(row == rows - 1, 0.0, pltpu.roll(r, rows - 1, axis=0))
    return jnp.where(lane == LANES - 1, edge, r)


def _hyena_kernel(g0a_ref, g1a_ref, za_ref, g0b_ref, g1b_ref, zb_ref, cw0_ref, cw1_ref, cwz_ref, skip_ref,
                  h0r_ref, h0i_ref, h1r_ref, h1i_ref, f1m_ref, g1m_ref, f2m_ref, g2m_ref, twr_ref, twi_ref,
                  oa_ref, ob_ref, z_scr, g0_scr, g1_scr, a_scr, b_scr, *, n1, cb):
    nh = n1 // 2

    def short_conv(src_a, src_b, cw_ref, dst):
        for c in range(cb):
            sl = slice(c * LANES, (c + 1) * LANES)
            for half, src in enumerate((src_a, src_b)):
                x = src[0, :, sl]
                y = (cw_ref[3:4, sl] + cw_ref[0:1, sl] * _shift_time(x, 1) + cw_ref[1:2, sl] * x
                     + cw_ref[2:3, sl] * _shift_time(x, -1))
                dst[half * nh:(half + 1) * nh, sl] = y

    short_conv(g0a_ref, g0b_ref, cw0_ref, g0_scr)
    short_conv(g1a_ref, g1b_ref, cw1_ref, g1_scr)
    short_conv(za_ref, zb_ref, cwz_ref, z_scr)
    tr, ti = twr_ref[...], twi_ref[...]

    def long_conv(hr_ref, hi_ref):
        a_scr[...] = _dot(f1m_ref[...], z_scr[...].astype(BF16))
        for c in range(cb):
            sl = slice(c * LANES, (c + 1) * LANES)
            ar, ai = a_scr[0:n1, sl], a_scr[n1:2 * n1, sl]
            cat = jnp.concatenate([ar * tr - ai * ti, ar * ti + ai * tr], axis=1).astype(BF16)
            x = _dot(cat, f2m_ref[...])
            xr, xi = x[:, :LANES], x[:, LANES:]
            hr, hi = hr_ref[:, sl], hi_ref[:, sl]
            cat = jnp.concatenate([xr * hr - xi * hi, xr * hi + xi * hr], axis=1).astype(BF16)
            y = _dot(cat, g2m_ref[...])
            br, bi = y[:, :LANES], y[:, LANES:]
            b_scr[0:n1, sl] = br * tr + bi * ti
            b_scr[n1:2 * n1, sl] = bi * tr - br * ti
        return _dot(g1m_ref[...], b_scr[...].astype(BF16))

    z0 = z_scr[...]
    z1 = g0_scr[...] * (long_conv(h0r_ref, h0i_ref) + z0 * skip_ref[0:1, :])
    z_scr[...] = z1
    z2 = g1_scr[...] * (long_conv(h1r_ref, h1i_ref) + z1 * skip_ref[1:2, :])
    oa_ref[0] = z2[:nh]
    ob_ref[0] = z2[nh:]


def _hyena(but, hr, hi, conv_w, conv_b, skip, L, cb):
    bsz = but.shape[0]
    tabs = _dft_tables(L)
    n1 = tabs["n1"]
    nh = n1 // 2
    W = cb * LANES
    ncb = B_WIDTH // cb
    npair = bsz // 2
    cw = jnp.repeat(jnp.concatenate([conv_w.astype(F32), conv_b.astype(F32)[None]], axis=0), LANES, axis=1)
    sk = jnp.repeat(skip.astype(F32), LANES, axis=1)

    def data_spec(group, second):
        return pl.BlockSpec((1, nh, W), lambda j, p: (p + second * npair, 0, group * ncb + j))

    def cw_spec(group):
        return pl.BlockSpec((4, W), lambda j, p: (0, group * ncb + j))

    def h_spec(order):
        return pl.BlockSpec((n1, W), lambda j, p: (0, order * ncb + j))

    const = lambda j, p: (0, 0)
    in_specs = ([data_spec(0, 0), data_spec(1, 0), data_spec(2, 0), data_spec(0, 1), data_spec(1, 1), data_spec(2, 1),
                 cw_spec(0), cw_spec(1), cw_spec(2),
                 pl.BlockSpec((2, W), lambda j, p: (0, j)),
                 h_spec(0), h_spec(0), h_spec(1), h_spec(1),
                 pl.BlockSpec((2 * n1, n1), const), pl.BlockSpec((n1, 2 * n1), const),
                 pl.BlockSpec((2 * LANES, 2 * LANES), const), pl.BlockSpec((2 * LANES, 2 * LANES), const),
                 pl.BlockSpec((n1, LANES), const), pl.BlockSpec((n1, LANES), const)])
    oa, ob = pl.pallas_call(
        functools.partial(_hyena_kernel, n1=n1, cb=cb),
        grid=(ncb, npair),
        in_specs=in_specs,
        out_specs=[pl.BlockSpec((1, nh, W), lambda j, p: (p, 0, j)),
                   pl.BlockSpec((1, nh, W), lambda j, p: (p, 0, j))],
        out_shape=[jax.ShapeDtypeStruct((npair, nh, B_WIDTH * LANES), F32),
                   jax.ShapeDtypeStruct((npair, nh, B_WIDTH * LANES), F32)],
        scratch_shapes=[pltpu.VMEM((n1, W), F32), pltpu.VMEM((n1, W), F32), pltpu.VMEM((n1, W), F32),
                        pltpu.VMEM((2 * n1, W), F32), pltpu.VMEM((2 * n1, W), F32)],
        compiler_params=_params("arbitrary", "arbitrary"),
        name="hyena",
    )(but, but, but, but, but, but, cw, cw, cw, sk, hr, hi, hr, hi,
      jnp.asarray(tabs["f1m"], BF16), jnp.asarray(tabs["g1m"], BF16),
      jnp.asarray(tabs["f2m"], BF16), jnp.asarray(tabs["g2m"], BF16),
      jnp.asarray(tabs["twr"], F32), jnp.asarray(tabs["twi"], F32))
    return jnp.concatenate([oa, ob], axis=0)


def _cumsum_rows(x):
    rows = x.shape[0]
    row = lax.broadcasted_iota(jnp.int32, x.shape, 0)
    k = 1
    while k < rows:
        x = x + jnp.where(row >= k, pltpu.roll(x, k, axis=0), 0.0)
        k *= 2
    return x


def _gla_chunk(q, z, v, lb, st_ref, o_ref, rows, forward):
    C = GLA_CHUNK
    mid = C // 2
    one_m = 1.0 - lb
    f = lb + one_m * _sigmoid(z)
    logf = jnp.log(jnp.maximum(f, F_FLOOR))
    kk = one_m * _sigmoid(-z)
    cum = _cumsum_rows(logf)
    tot = cum[C - 1:C]
    if forward:
        acc = cum
        anchor = cum[mid - 1:mid]
    else:
        acc = tot - cum + logf
        anchor = acc[mid:mid + 1]
    qt = q * jnp.exp(acc - anchor)
    kt = kk * jnp.exp(anchor - acc)
    qi = q * jnp.exp(acc)
    ks = kk * jnp.exp(tot - acc)
    dec = jnp.exp(tot)
    ti = lax.broadcasted_iota(jnp.int32, (C, C), 0)
    si = lax.broadcasted_iota(jnp.int32, (C, C), 1)
    causal = (ti >= si) if forward else (si >= ti)
    lane = lax.broadcasted_iota(jnp.int32, (C, LANES), 1)
    first = lane < C_HEAD_DIM
    r2 = lax.broadcasted_iota(jnp.int32, (LANES, LANES), 0) < C_HEAD_DIM
    c2 = lax.broadcasted_iota(jnp.int32, (LANES, LANES), 1) < C_HEAD_DIM
    same_head = r2 == c2
    for p in range(C_WIDTH // LANES):
        sl = slice(p * LANES, (p + 1) * LANES)
        qtp, ktp, vp = qt[:, sl], kt[:, sl].astype(BF16), v[:, sl]
        s0 = jnp.where(causal, _dot_nt(jnp.where(first, qtp, 0.0).astype(BF16), ktp), 0.0).astype(BF16)
        s1 = jnp.where(causal, _dot_nt(jnp.where(first, 0.0, qtp).astype(BF16), ktp), 0.0).astype(BF16)
        intra = (_dot(s0, jnp.where(first, vp, 0.0).astype(BF16))
                 + _dot(s1, jnp.where(first, 0.0, vp).astype(BF16)))
        st = st_ref[p]
        inter = _dot_nt(qi[:, sl].astype(BF16), st.astype(BF16))
        o_ref[0, rows, sl] = intra + inter
        upd = _dot_tn(vp.astype(BF16), ks[:, sl].astype(BF16))
        st_ref[p] = st * dec[:, sl] + jnp.where(same_head, upd, 0.0)


def _hgrn_kernel(qf_ref, zf_ref, vf_ref, qb_ref, zb_ref, vb_ref, lb_ref, of_ref, ob_ref, stf, stb, *, nsub):
    @pl.when(pl.program_id(1) == 0)
    def _():
        stf[...] = jnp.zeros_like(stf)
        stb[...] = jnp.zeros_like(stb)

    lbf, lbb = lb_ref[0:1, :], lb_ref[1:2, :]

    def body(j, carry):
        rf = pl.ds(pl.multiple_of(j * GLA_CHUNK, GLA_CHUNK), GLA_CHUNK)
        _gla_chunk(qf_ref[0, rf, :], zf_ref[0, rf, :], vf_ref[0, rf, :], lbf, stf, of_ref, rf, True)
        rb = pl.ds(pl.multiple_of((nsub - 1 - j) * GLA_CHUNK, GLA_CHUNK), GLA_CHUNK)
        _gla_chunk(qb_ref[0, rb, :], zb_ref[0, rb, :], vb_ref[0, rb, :], lbb, stb, ob_ref, rb, False)
        return carry

    lax.fori_loop(0, nsub, body, 0)


def _hgrn(c_in, lb, rb):
    bsz, L, _ = c_in.shape
    nblk = L // rb
    w = C_WIDTH
    fwd = lambda col: pl.BlockSpec((1, rb, w), lambda b, i: (b, i, col))
    bwd = lambda col: pl.BlockSpec((1, rb, w), lambda b, i: (b, nblk - 1 - i, col))
    npair = w // LANES
    return pl.pallas_call(
        functools.partial(_hgrn_kernel, nsub=rb // GLA_CHUNK),
        grid=(bsz, nblk),
        in_specs=[fwd(0), fwd(1), fwd(3), bwd(0), bwd(2), bwd(3), pl.BlockSpec((2, w), lambda b, i: (0, 0))],
        out_specs=[pl.BlockSpec((1, rb, w), lambda b, i: (b, i, 0)),
                   pl.BlockSpec((1, rb, w), lambda b, i: (b, nblk - 1 - i, 0))],
        out_shape=[jax.ShapeDtypeStruct((bsz, L, w), F32), jax.ShapeDtypeStruct((bsz, L, w), F32)],
        scratch_shapes=[pltpu.VMEM((npair, LANES, LANES), F32), pltpu.VMEM((npair, LANES, LANES), F32)],
        compiler_params=_params("arbitrary", "arbitrary"),
        name="hgrn",
    )(c_in, c_in, c_in, c_in, c_in, c_in, lb)


def _gelu_tanh(x):
    return 0.5 * x * (1.0 + jnp.tanh(math.sqrt(2.0 / math.pi) * (x + 0.044715 * (x * x * x))))


def _out_kernel(x_ref, mod_ref, ys_ref, au_ref, g_ref, hy_ref, of_ref, ob_ref, d_ref, gw_ref, gb_ref, hn_ref,
                ones_ref, wo_ref, fn_ref, o_ref, *, final):
    g = g_ref[0]
    y = ys_ref[0] + au_ref[0] * d_ref[...]
    z = _gelu_tanh(y)
    a_out = z * _sigmoid(_dot(z.astype(BF16), gw_ref[...]) + gb_ref[...]) * _silu(g[:, :A_WIDTH])
    b_out = hy_ref[0] * _silu(g[:, A_WIDTH:A_WIDTH + B_WIDTH])
    o = of_ref[0] + ob_ref[0]
    o2 = o * o
    hi = o2.astype(BF16)
    lo = (o2 - hi.astype(F32)).astype(BF16)
    ms = (_dot(hi, ones_ref[...]) + _dot(lo, ones_ref[...])) * (1.0 / C_HEAD_DIM)
    c_out = o * lax.rsqrt(ms + EPS) * hn_ref[...] * _silu(g[:, A_WIDTH + B_WIDTH:])
    mixed = _dot(jnp.concatenate([a_out, b_out, c_out], axis=1).astype(BF16), wo_ref[...])
    xn = x_ref[0] + mod_ref[0, 2:3, :] * mixed
    if final:
        ms = jnp.mean(xn * xn, axis=-1, keepdims=True)
        xn = xn * lax.rsqrt(ms + EPS) * fn_ref[...]
    o_ref[0] = xn


def _out(x, mod, ys, au, gates, hy, of, ob, d, gw, gb, hn, ones_bd, wo, fn, tb, final):
    bsz, L, dm = x.shape
    row = lambda w: pl.BlockSpec((1, tb, w), lambda b, i: (b, i, 0))
    const = lambda b, i: (0, 0)
    full = lambda a: pl.BlockSpec(a.shape, const)
    return pl.pallas_call(
        functools.partial(_out_kernel, final=final),
        grid=(bsz, L // tb),
        in_specs=[row(dm), pl.BlockSpec((1, 3, dm), lambda b, i: (b, 0, 0)), row(A_WIDTH), row(A_WIDTH), row(dm),
                  row(B_WIDTH), row(C_WIDTH), row(C_WIDTH), full(d), full(gw), full(gb), full(hn), full(ones_bd),
                  full(wo), full(fn)],
        out_specs=row(dm),
        out_shape=jax.ShapeDtypeStruct((bsz, L, dm), F32),
        compiler_params=_params("arbitrary", "arbitrary"),
        name="out",
    )(x, mod, ys, au, gates, hy, of, ob, d, gw, gb, hn, ones_bd, wo, fn)


def _trunk(x, ada_rows, p, tb=256, hy_cb=8, hg_rb=256):
    bsz, L, _ = x.shape
    head = np.arange(C_WIDTH) // C_HEAD_DIM
    ones_bd = jnp.asarray(head[:, None] == head[None, :], BF16)
    for layer in range(DEPTH):
        mod = ada_rows[layer].reshape(bsz, 3, D_MODEL)
        w_in = p["w_in"][layer]
        o_a, o_b, o_c = 2 * A_WIDTH, 2 * A_WIDTH + 4 * B_WIDTH, 2 * A_WIDTH + 4 * B_WIDTH + 5 * C_WIDTH
        wau = w_in[:, :A_WIDTH].astype(BF16)
        wbt = w_in[:, o_a:o_a + 3 * B_WIDTH].T.astype(BF16)
        wc = w_in[:, o_b:o_b + 4 * C_WIDTH].astype(BF16)
        wg = jnp.concatenate([w_in[:, A_WIDTH:o_a], w_in[:, o_a + 3 * B_WIDTH:o_b], w_in[:, o_b + 4 * C_WIDTH:o_c]],
                             axis=1).astype(BF16)
        au, c_in, gates, but = _proj(x, mod, p["norm_w"][layer].reshape(1, -1).astype(F32), wau, wc, wg, wbt, tb)

        ys = _from_chunk_rows(_s5(_to_chunk_rows(au), p["s5_consts"][layer]), bsz, L)

        hr, hi = p["hy_filters"][(layer, L)]
        hyt = _hyena(but.reshape(bsz, L // LANES, 3 * B_WIDTH * LANES), hr, hi, p["hy_conv_w"][layer],
                     p["hy_conv_b"][layer], p["hy_skip"][layer], L, hy_cb)
        hy = hyt.reshape(bsz, L // LANES, B_WIDTH, LANES).transpose(0, 1, 3, 2).reshape(bsz, L, B_WIDTH)

        of, ob = _hgrn(c_in, p["lower_bounds"][layer], hg_rb)

        x = _out(x, mod, ys, au, gates, hy, of, ob,
                 p["s5_d"][layer].reshape(1, -1).astype(F32), p["s5_glu_w"][layer].astype(BF16),
                 p["s5_glu_b"][layer].reshape(1, -1).astype(F32),
                 jnp.tile(p["hg_norm_w"][layer].astype(F32), C_WIDTH // C_HEAD_DIM).reshape(1, -1),
                 ones_bd, p["w_out"][layer].astype(BF16), p["final_norm_w"].reshape(1, -1).astype(F32),
                 tb, layer == DEPTH - 1)
    return x


def kernel(x_prompt, x_sample, c_prompt, c_sample, norm_w, ada_w, ada_b, w_in, w_out, s5_lambda_re, s5_lambda_im, s5_log_dt, s5_b_re, s5_b_im, s5_c_re, s5_c_im, s5_d, s5_glu_w, s5_glu_b, hy_conv_w, hy_conv_b, hy_w1, hy_b1, hy_w2, hy_b2, hy_w3, hy_sin_freq, hy_log_decay, hy_skip, hg_lb_logits, hg_norm_w, final_norm_w):
    bp, bs = x_prompt.shape[0], x_sample.shape[0]
    ada = _ada(jnp.concatenate([c_prompt, c_sample], axis=0).astype(F32), ada_w.astype(F32), ada_b.astype(F32))
    lb_soft = jax.nn.softmax(hg_lb_logits.astype(F32), axis=0)
    lower_bounds = jnp.cumsum(lb_soft, axis=0) - lb_soft[0:1]
    lengths = sorted({x_prompt.shape[1], x_sample.shape[1]})
    p = dict(
        norm_w=norm_w, w_in=w_in, w_out=w_out, s5_d=s5_d, s5_glu_w=s5_glu_w, s5_glu_b=s5_glu_b,
        hy_conv_w=hy_conv_w, hy_conv_b=hy_conv_b, hy_skip=hy_skip, hg_norm_w=hg_norm_w, final_norm_w=final_norm_w,
        lower_bounds=lower_bounds,
        s5_consts=[_s5_consts(s5_lambda_re[l], s5_lambda_im[l], s5_log_dt[l], s5_b_re[l], s5_b_im[l],
                              s5_c_re[l], s5_c_im[l]) for l in range(DEPTH)],
        hy_filters={(l, L): _hyena_filters(L, hy_w1[l], hy_b1[l], hy_w2[l], hy_b2[l], hy_w3[l], hy_sin_freq[l],
                                           hy_log_decay[l], 8)
                    for l in range(DEPTH) for L in lengths},
    )
    y_prompt = _trunk(x_prompt, ada[:, :bp], p)
    y_sample = _trunk(x_sample, ada[:, bp:], p)
    return (y_prompt, y_sample)
```

```python
import functools
import math

import numpy as np
import jax
import jax.numpy as jnp
from jax import lax
from jax.experimental import pallas as pl
from jax.experimental.pallas import tpu as pltpu

F32 = jnp.float32
BF16 = jnp.bfloat16
HIGHEST = lax.Precision.HIGHEST

D_MODEL = 1024
DEPTH = 2
A_WIDTH = 256
A_GROUP_CH = 16
A_GROUPS = 16
A_STATE = 64
B_WIDTH = 384
HY_ORDER = 2
HY_BANDS = 16
HY_POS_FEAT = 1 + 2 * HY_BANDS
HY_FILT_HID = 64
C_WIDTH = 384
C_HEAD_DIM = 64
EPS = 1e-6
F_FLOOR = 1e-30

LANES = 128
SUBLANES = 8
GLA_CHUNK = 128
VMEM_LIMIT = 56 * 1024 * 1024

CT_AU, CT_AG, CT_BU, CT_BG = 0, A_WIDTH, 2 * A_WIDTH, 2 * A_WIDTH + 3 * B_WIDTH
CT_ROWS = 2 * A_WIDTH + 4 * B_WIDTH
NAT_COLS = 5 * C_WIDTH


def _dot(a, b, precision=None):
    return jnp.dot(a, b, preferred_element_type=F32, precision=precision)


def _dot_nt(a, b, precision=None):
    return lax.dot_general(a, b, (((1,), (1,)), ((), ())), preferred_element_type=F32, precision=precision)


def _dot_tn(a, b, precision=None):
    return lax.dot_general(a, b, (((0,), (0,)), ((), ())), preferred_element_type=F32, precision=precision)


def _split_bf16(a):
    hi = a.astype(BF16)
    return hi, (a - hi.astype(F32)).astype(BF16)


def _sigmoid(x):
    return 1.0 / (1.0 + jnp.exp(-x))


def _silu(x):
    return x * _sigmoid(x)


def _params(*sem):
    return pltpu.CompilerParams(dimension_semantics=sem, vmem_limit_bytes=VMEM_LIMIT)


def _resident(shape, index_map):
    return pl.BlockSpec(shape, index_map, pipeline_mode=pl.Buffered(1))


def _ada_kernel(c_ref, w_ref, b_ref, o_ref):
    cond = _silu(c_ref[...])
    o_ref[0] = _dot(cond, w_ref[0], precision=HIGHEST) + b_ref[0]


def _ada(c_all, ada_w, ada_b):
    nb = c_all.shape[0]
    d = D_MODEL
    return pl.pallas_call(
        _ada_kernel,
        grid=(DEPTH, 3),
        in_specs=[pl.BlockSpec((nb, d), lambda l, j: (0, 0)),
                  pl.BlockSpec((1, d, d), lambda l, j: (l, 0, j)),
                  pl.BlockSpec((1, 1, d), lambda l, j: (l, 0, j))],
        out_specs=pl.BlockSpec((1, nb, d), lambda l, j: (l, 0, j)),
        out_shape=jax.ShapeDtypeStruct((DEPTH, nb, 3 * d), F32),
        compiler_params=_params("arbitrary", "arbitrary"),
        name="ada",
    )(c_all, ada_w, ada_b.reshape(DEPTH, 1, 3 * d))


def _proj_kernel(x_ref, mod_ref, nw_ref, wt_ref, wc_ref, pt_ref, c_ref):
    x = x_ref[...]
    bsz = x.shape[0]
    ms = jnp.mean(x * x, axis=-1, keepdims=True)
    h = x * lax.rsqrt(ms + EPS) * nw_ref[...]
    h = h * (1.0 + mod_ref[:, 1:2, :]) + mod_ref[:, 0:1, :]
    hb = h.astype(BF16).reshape(bsz * LANES, D_MODEL)
    rows = 256
    for r in range(0, CT_ROWS, rows):
        pt_ref[0, r:r + rows, :] = _dot_nt(wt_ref[r:r + rows, :], hb)
    for k in range(NAT_COLS // C_WIDTH):
        sl = slice(k * C_WIDTH, (k + 1) * C_WIDTH)
        c_ref[:, :, sl] = _dot(hb, wc_ref[:, sl]).reshape(bsz, LANES, C_WIDTH)


def _proj(x, mod, norm_w, wt, wc):
    bsz, L, d = x.shape
    nchunk = L // LANES
    return pl.pallas_call(
        _proj_kernel,
        grid=(nchunk,),
        in_specs=[pl.BlockSpec((bsz, LANES, d), lambda i: (0, i, 0)),
                  _resident((bsz, 3, d), lambda i: (0, 0, 0)),
                  _resident((1, d), lambda i: (0, 0)),
                  _resident(wt.shape, lambda i: (0, 0)),
                  _resident(wc.shape, lambda i: (0, 0))],
        out_specs=[pl.BlockSpec((1, CT_ROWS, bsz * LANES), lambda i: (i, 0, 0)),
                   pl.BlockSpec((bsz, LANES, NAT_COLS), lambda i: (0, i, 0))],
        out_shape=[jax.ShapeDtypeStruct((nchunk, CT_ROWS, bsz * LANES), F32),
                   jax.ShapeDtypeStruct((bsz, L, NAT_COLS), F32)],
        compiler_params=_params("arbitrary"),
        name="proj",
    )(x, mod, norm_w, wt, wc)


def _s5_consts(lam_re, lam_im, log_dt, b_re, b_im, c_re, c_im):
    T = LANES
    G, P, H = A_GROUPS, A_STATE, A_GROUP_CH
    dt = jnp.exp(log_dt.astype(F32))[..., None]
    lr, li = lam_re.astype(F32) * dt, lam_im.astype(F32) * dt

    def power(k):
        mag = jnp.exp(lr * k)
        return mag * jnp.cos(li * k), mag * jnp.sin(li * k)

    def cmul(ar, ai, br, bi):
        return ar * br - ai * bi, ar * bi + ai * br

    p1r, p1i = power(1.0)
    den = lam_re.astype(F32) ** 2 + lam_im.astype(F32) ** 2
    cfr, cfi = cmul(p1r - 1.0, p1i, lam_re.astype(F32) / den, -lam_im.astype(F32) / den)
    bbr, bbi = cmul(cfr[..., None], cfi[..., None], b_re.astype(F32)[None], b_im.astype(F32)[None])
    cr, ci = c_re.astype(F32), c_im.astype(F32)

    k = jnp.arange(T + 1, dtype=F32)[:, None, None, None]
    pwr, pwi = power(k)
    cpr, cpi = cmul(cr[None], ci[None], pwr[:T, :, :, None, :], pwi[:T, :, :, None, :])
    kern = (jnp.einsum("kdghp,dgpi->dkghi", cpr, bbr, precision=HIGHEST)
            - jnp.einsum("kdghp,dgpi->dkghi", cpi, bbi, precision=HIGHEST))
    two = jnp.concatenate([kern[1][:0:-1], (kern[0][0] + kern[1][0])[None], kern[0][1:],
                           jnp.zeros((1,) + kern.shape[2:], F32)], axis=0)
    k2 = two.transpose(1, 3, 2, 0).reshape(G, H * H, 2 * T)

    sf_r, sf_i = cmul(pwr[:T][::-1, 0, :, :, None], pwi[:T][::-1, 0, :, :, None], bbr[0][None], bbi[0][None])
    sb_r, sb_i = cmul(pwr[:T, 1, :, :, None], pwi[:T, 1, :, :, None], bbr[1][None], bbi[1][None])
    bmat = jnp.concatenate([sf_r, sb_r, sf_i, sb_i], axis=2)
    bmat = bmat.transpose(1, 3, 0, 2).reshape(G, H * T, 4 * P)

    mf_r, mf_i = cmul(cr[0][None], ci[0][None], pwr[1:, 0, :, None, :], pwi[1:, 0, :, None, :])
    mb_r, mb_i = cmul(cr[1][None], ci[1][None], pwr[1:][::-1, 1, :, None, :], pwi[1:][::-1, 1, :, None, :])
    zero = jnp.zeros_like(mf_r)
    cf = jnp.concatenate([mf_r, zero, -mf_i, zero], axis=3)
    cb = jnp.concatenate([zero, mb_r, zero, -mb_i], axis=3)
    cf = cf.transpose(1, 3, 2, 0).reshape(G, 4 * P, H * T)
    cb = cb.transpose(1, 3, 2, 0).reshape(G, 4 * P, H * T)

    lam_t = jnp.stack([jnp.concatenate([pwr[T, 0], pwr[T, 1]], axis=-1),
                       jnp.concatenate([pwi[T, 0], pwi[T, 1]], axis=-1)], axis=1)
    return k2, bmat.astype(BF16), cf.astype(BF16), cb.astype(BF16), lam_t


def _s5_kernel(u_ref, k2_ref, b_ref, cf_ref, cb_ref, lam_ref, y_ref, tt, a_scr, sr_scr, si_scr, xf_scr, xb_scr):
    nchunk = u_ref.shape[0]
    m = nchunk * SUBLANES
    H, T = A_GROUP_CH, LANES

    def toeplitz(hp, carry):
        rows = k2_ref[0, pl.ds(pl.multiple_of(hp * H, H), H), :]
        for h in range(H):
            x = jnp.broadcast_to(rows[h:h + 1, :], (T, 2 * T))
            r = pltpu.roll(x, 1, axis=1, stride=1, stride_axis=0)
            tt[pl.ds(pl.multiple_of(hp * T, T), T), h * T:(h + 1) * T] = r[:, T:].astype(BF16)
        return carry

    lax.fori_loop(0, H, toeplitz, 0)

    for hp in range(H):
        a_scr[:, hp * T:(hp + 1) * T] = u_ref[:, hp].reshape(m, T).astype(BF16)
    s = _dot(a_scr[...], b_ref[0])
    sr_scr[...] = s[:, :LANES]
    si_scr[...] = s[:, LANES:]

    lam_r = jnp.broadcast_to(lam_ref[0, 0:1, :], (SUBLANES, LANES))
    lam_i = jnp.broadcast_to(lam_ref[0, 1:2, :], (SUBLANES, LANES))
    fwd_lane = lax.broadcasted_iota(jnp.int32, (SUBLANES, LANES), 1) < A_STATE

    def step(i, carry):
        xr, xi = carry
        rf = pl.ds(pl.multiple_of(i * SUBLANES, SUBLANES), SUBLANES)
        rb = pl.ds(pl.multiple_of((nchunk - 1 - i) * SUBLANES, SUBLANES), SUBLANES)
        xf_scr[rf, 0:LANES] = xr
        xf_scr[rf, LANES:2 * LANES] = xi
        xb_scr[rb, 0:LANES] = xr
        xb_scr[rb, LANES:2 * LANES] = xi
        sr = jnp.where(fwd_lane, sr_scr[rf, :], sr_scr[rb, :])
        si = jnp.where(fwd_lane, si_scr[rf, :], si_scr[rb, :])
        return lam_r * xr - lam_i * xi + sr, lam_r * xi + lam_i * xr + si

    zero = jnp.zeros((SUBLANES, LANES), F32)
    lax.fori_loop(0, nchunk, step, (zero, zero))

    a = a_scr[...]
    xf = xf_scr[...].astype(BF16)
    xb = xb_scr[...].astype(BF16)
    for j in range(H // 2):
        sl = slice(j * 2 * T, (j + 1) * 2 * T)
        yj = _dot(a, tt[:, sl]) + _dot(xf, cf_ref[0, :, sl]) + _dot(xb, cb_ref[0, :, sl])
        y_ref[:, 2 * j] = yj[:, :T].reshape(nchunk, SUBLANES, T)
        y_ref[:, 2 * j + 1] = yj[:, T:].reshape(nchunk, SUBLANES, T)


def _s5(pt4, consts):
    k2, bmat, cf, cb, lam_t = consts
    nchunk, _, bsz, _ = pt4.shape
    assert bsz == SUBLANES
    G, H, T = A_GROUPS, A_GROUP_CH, LANES
    m = nchunk * bsz
    per_g = lambda g: (g, 0, 0)
    return pl.pallas_call(
        _s5_kernel,
        grid=(G,),
        in_specs=[pl.BlockSpec((nchunk, H, bsz, T), lambda g: (0, g, 0, 0)),
                  pl.BlockSpec((1, H * H, 2 * T), per_g),
                  pl.BlockSpec((1, H * T, 4 * A_STATE), per_g),
                  pl.BlockSpec((1, 4 * A_STATE, H * T), per_g),
                  pl.BlockSpec((1, 4 * A_STATE, H * T), per_g),
                  pl.BlockSpec((1, 2, LANES), per_g)],
        out_specs=pl.BlockSpec((nchunk, H, bsz, T), lambda g: (0, g, 0, 0)),
        out_shape=jax.ShapeDtypeStruct((nchunk, A_WIDTH, bsz, T), F32),
        scratch_shapes=[pltpu.VMEM((H * T, H * T), BF16), pltpu.VMEM((m, H * T), BF16),
                        pltpu.VMEM((m, LANES), F32), pltpu.VMEM((m, LANES), F32),
                        pltpu.VMEM((m, 2 * LANES), F32), pltpu.VMEM((m, 2 * LANES), F32)],
        compiler_params=_params("arbitrary"),
        name="s5",
    )(pt4, k2, bmat, cf, cb, lam_t)


def _dft_tables(L):
    N = 2 * L
    n2 = LANES
    n1 = N // n2
    h = n1 // 2
    k = np.arange(n1)
    f1 = np.exp(-2j * np.pi * np.outer(k, k) / n1)
    f2 = np.exp(-2j * np.pi * np.outer(np.arange(n2), np.arange(n2)) / n2)
    tw = np.exp(-2j * np.pi * np.outer(k, np.arange(n2)) / N)
    f1m = np.block([[f1.real[:, :h], -f1.imag[:, :h]], [f1.imag[:, :h], f1.real[:, :h]]])
    ffm = np.concatenate([f1.real, f1.imag], axis=0)
    f2m = np.block([[f2.real, f2.imag], [-f2.imag, f2.real]])
    g2m = np.block([[f2.real, -f2.imag], [f2.imag, f2.real]])
    g1 = np.conj(f1)[:h, :]
    g1m = np.block([[g1.real, -g1.imag], [g1.imag, g1.real]])
    return dict(n1=n1, f1m=f1m, ffm=ffm, f2m=f2m, g2m=g2m, g1m=g1m, twr=tw.real, twi=tw.imag)


def _filter_positions(L):
    N = 2 * L
    i = np.arange(N)
    pos = np.where(i < L, i, N - i).astype(np.float32)
    t = pos / np.float32(max(L - 1, 1))
    w = np.float32(2.0 * math.pi) * pos / np.float32(L)
    bands = np.linspace(1e-4, HY_BANDS - 1, HY_BANDS, dtype=np.float32)
    wb = (w[:, None] * bands).astype(np.float32)
    feats = np.concatenate([t[:, None], np.cos(wb), -np.sin(wb)], axis=-1).astype(np.float32)
    fpad = np.zeros((N, LANES), np.float32)
    fpad[:, :HY_POS_FEAT] = feats
    keep = np.ones((N,), np.float32)
    keep[L] = 0.0
    n1 = N // LANES
    return fpad, t.reshape(n1, 1, LANES), keep.reshape(n1, 1, LANES)


def _filt_kernel(feat_ref, w1_ref, b1_ref, w2_ref, b2_ref, w3t_ref, sf_ref, ld_ref, t_ref, keep_ref,
                 ts_ref, scale_ref, ss_scr, *, inv_n):
    f = feat_ref[...]
    h1 = jnp.sin(sf_ref[0:1, :] * (_dot(f, w1_ref[...], precision=HIGHEST) + b1_ref[...]))
    h2 = jnp.sin(sf_ref[1:2, :] * (_dot(h1, w2_ref[...], precision=HIGHEST) + b2_ref[...]))
    w_hi, w_lo = _split_bf16(w3t_ref[0])
    h_hi, h_lo = _split_bf16(h2)
    h3t = _dot_nt(w_hi, h_hi) + _dot_nt(w_hi, h_lo) + _dot_nt(w_lo, h_hi)
    win = jnp.exp(-jnp.exp(ld_ref[0]) * t_ref[0])
    ts = h3t * win * keep_ref[0]
    ts_ref[0] = ts

    @pl.when(pl.program_id(0) == 0)
    def _():
        ss_scr[...] = jnp.zeros_like(ss_scr)

    ss_scr[...] += ts * ts

    @pl.when(pl.program_id(0) == pl.num_programs(0) - 1)
    def _():
        tot = jnp.sum(ss_scr[...], axis=-1, keepdims=True)
        scale_ref[...] = lax.rsqrt(tot + EPS) * inv_n


def _filt_fft_kernel(ts_ref, ffm_ref, twr_ref, twi_ref, f2m_ref, hr_ref, hi_ref, *, n1, cb):
    a = _dot(ffm_ref[...], ts_ref[...].astype(BF16))
    tr, ti = twr_ref[...], twi_ref[...]
    for c in range(cb):
        sl = slice(c * LANES, (c + 1) * LANES)
        ar, ai = a[:n1, sl], a[n1:, sl]
        cat = jnp.concatenate([ar * tr - ai * ti, ar * ti + ai * tr], axis=1).astype(BF16)
        x = _dot(cat, f2m_ref[...])
        hr_ref[:, sl] = x[:, :LANES]
        hi_ref[:, sl] = x[:, LANES:]


def _hyena_filters(L, w1, b1, w2, b2, w3, sin_freq, log_decay, cb=8):
    tabs = _dft_tables(L)
    n1 = tabs["n1"]
    N = 2 * L
    nch = HY_ORDER * B_WIDTH
    fpad, tpos, keep = _filter_positions(L)
    w1p = jnp.zeros((LANES, HY_FILT_HID), F32).at[:HY_POS_FEAT].set(w1.astype(F32))
    w3t = w3.astype(F32).T.reshape(2, nch, HY_FILT_HID)
    ld = log_decay.astype(F32).reshape(2, nch, 1)
    half = n1 // 2
    ts, scale = pl.pallas_call(
        functools.partial(_filt_kernel, inv_n=1.0 / N),
        grid=(n1,),
        in_specs=[pl.BlockSpec((LANES, LANES), lambda i: (i, 0)),
                  pl.BlockSpec((LANES, HY_FILT_HID), lambda i: (0, 0)),
                  pl.BlockSpec((1, HY_FILT_HID), lambda i: (0, 0)),
                  pl.BlockSpec((HY_FILT_HID, HY_FILT_HID), lambda i: (0, 0)),
                  pl.BlockSpec((1, HY_FILT_HID), lambda i: (0, 0)),
                  pl.BlockSpec((1, nch, HY_FILT_HID), lambda i: (i // half, 0, 0)),
                  pl.BlockSpec((2, HY_FILT_HID), lambda i: (0, 0)),
                  pl.BlockSpec((1, nch, 1), lambda i: (i // half, 0, 0)),
                  pl.BlockSpec((1, 1, LANES), lambda i: (i, 0, 0)),
                  pl.BlockSpec((1, 1, LANES), lambda i: (i, 0, 0))],
        out_specs=[pl.BlockSpec((1, nch, LANES), lambda i: (i, 0, 0)),
                   pl.BlockSpec((nch, 1), lambda i: (0, 0))],
        out_shape=[jax.ShapeDtypeStruct((n1, nch, LANES), F32),
                   jax.ShapeDtypeStruct((nch, 1), F32)],
        scratch_shapes=[pltpu.VMEM((nch, LANES), F32)],
        compiler_params=_params("arbitrary"),
        name="hy_filter",
    )(jnp.asarray(fpad), w1p, b1.astype(F32).reshape(1, -1), w2.astype(F32), b2.astype(F32).reshape(1, -1),
      w3t, sin_freq.astype(F32), ld, jnp.asarray(tpos), jnp.asarray(keep))
    W = cb * LANES
    hr, hi = pl.pallas_call(
        functools.partial(_filt_fft_kernel, n1=n1, cb=cb),
        grid=(nch // cb,),
        in_specs=[pl.BlockSpec((n1, W), lambda j: (0, j)),
                  pl.BlockSpec((2 * n1, n1), lambda j: (0, 0)),
                  pl.BlockSpec((n1, LANES), lambda j: (0, 0)),
                  pl.BlockSpec((n1, LANES), lambda j: (0, 0)),
                  pl.BlockSpec((2 * LANES, 2 * LANES), lambda j: (0, 0))],
        out_specs=[pl.BlockSpec((n1, W), lambda j: (0, j)),
                   pl.BlockSpec((n1, W), lambda j: (0, j))],
        out_shape=[jax.ShapeDtypeStruct((n1, nch * LANES), F32),
                   jax.ShapeDtypeStruct((n1, nch * LANES), F32)],
        compiler_params=_params("arbitrary"),
        name="hy_filter_fft",
    )(ts.reshape(n1, nch * LANES), jnp.asarray(tabs["ffm"], BF16), jnp.asarray(tabs["twr"], F32),
      jnp.asarray(tabs["twi"], F32), jnp.asarray(tabs["f2m"], BF16))
    scale_rep = jnp.repeat(scale.reshape(HY_ORDER, B_WIDTH), LANES, axis=1)
    return hr, hi, scale_rep


def _shift_time(x, direction):
    rows = x.shape[0]
    lane = lax.broadcasted_iota(jnp.int32, x.shape, 1)
    row = lax.broadcasted_iota(jnp.int32, x.shape, 0)
    if direction > 0:
        r = pltpu.roll(x, 1, axis=1)
        edge = jnp.where(row == 0, 0.0, pltpu.roll(r, 1, axis=0))
        return jnp.where(lane == 0, edge, r)
    r = pltpu.roll(x, LANES - 1, axis=1)
    edge = jnp.where(row == rows - 1, 0.0, pltpu.roll(r, rows - 1, axis=0))
    return jnp.where(lane == LANES - 1, edge, r)


def _hyena_kernel(g0_ref, g1_ref, z_ref, bg_ref, cw0_ref, cw1_ref, cwz_ref, skip_ref, scale_ref,
                  h0r_ref, h0i_ref, h1r_ref, h1i_ref, f1m_ref, g1m_ref, f2m_ref, g2m_ref, twr_ref, twi_ref,
                  o_ref, z_scr, g0_scr, g1_scr, a_scr, b_scr, *, n1, cb, bsz):
    nh = n1 // 2
    npair = bsz // 2
    tile = lambda i: slice(i * LANES, (i + 1) * LANES)

    def short_conv(src, cw_ref, dst):
        for c in range(cb):
            w0, w1, w2, bias = (cw_ref[k:k + 1, tile(c)] for k in range(4))
            for b in range(bsz):
                x = src[:, tile(c * bsz + b)]
                y = bias + w0 * _shift_time(x, 1) + w1 * x + w2 * _shift_time(x, -1)
                half, p = divmod(b, npair)
                dst[half * nh:(half + 1) * nh, tile(c * npair + p)] = y

    short_conv(g0_ref, cw0_ref, g0_scr)
    short_conv(g1_ref, cw1_ref, g1_scr)
    short_conv(z_ref, cwz_ref, z_scr)
    tr, ti = twr_ref[...], twi_ref[...]

    def long_conv(hr_ref, hi_ref):
        a_scr[...] = _dot(f1m_ref[...], z_scr[...].astype(BF16))
        for c in range(cb):
            hr, hi = hr_ref[:, tile(c)], hi_ref[:, tile(c)]
            for p in range(npair):
                sl = tile(c * npair + p)
                ar, ai = a_scr[0:n1, sl], a_scr[n1:2 * n1, sl]
                cat = jnp.concatenate([ar * tr - ai * ti, ar * ti + ai * tr], axis=1).astype(BF16)
                x = _dot(cat, f2m_ref[...])
                xr, xi = x[:, :LANES], x[:, LANES:]
                cat = jnp.concatenate([xr * hr - xi * hi, xr * hi + xi * hr], axis=1).astype(BF16)
                y = _dot(cat, g2m_ref[...])
                br, bi = y[:, :LANES], y[:, LANES:]
                b_scr[0:n1, sl] = br * tr + bi * ti
                b_scr[n1:2 * n1, sl] = bi * tr - br * ti
        return _dot(g1m_ref[...], b_scr[...].astype(BF16))

    def per_channel(ref, row):
        return jnp.concatenate([ref[row:row + 1, tile(c)] for c in range(cb) for _ in range(npair)], axis=1)

    z0 = z_scr[...]
    z1 = g0_scr[...] * (long_conv(h0r_ref, h0i_ref) * per_channel(scale_ref, 0) + z0 * per_channel(skip_ref, 0))
    z_scr[...] = z1
    z2 = g1_scr[...] * (long_conv(h1r_ref, h1i_ref) * per_channel(scale_ref, 1) + z1 * per_channel(skip_ref, 1))
    z_scr[...] = z2
    for c in range(cb):
        for b in range(bsz):
            half, p = divmod(b, npair)
            o_ref[:, tile(c * bsz + b)] = (z_scr[half * nh:(half + 1) * nh, tile(c * npair + p)]
                                           * _silu(bg_ref[:, tile(c * bsz + b)]))


def _hyena(pt2, filt, conv_w, conv_b, skip, L, bsz, cb=2):
    hr, hi, scale_rep = filt
    tabs = _dft_tables(L)
    n1 = tabs["n1"]
    nh = n1 // 2
    wd = cb * bsz * LANES
    wc = cb * LANES
    ws = cb * (bsz // 2) * LANES
    ncb = B_WIDTH // cb
    cw = jnp.repeat(jnp.concatenate([conv_w.astype(F32), conv_b.astype(F32)[None]], axis=0), LANES, axis=1)
    sk = jnp.repeat(skip.astype(F32), LANES, axis=1)
    data = lambda row0: pl.BlockSpec((nh, wd), lambda j: (0, row0 // cb + j))
    cws = lambda grp: pl.BlockSpec((4, wc), lambda j: (0, grp * ncb + j))
    hs = lambda order: pl.BlockSpec((n1, wc), lambda j: (0, order * ncb + j))
    const = lambda j: (0, 0)
    return pl.pallas_call(
        functools.partial(_hyena_kernel, n1=n1, cb=cb, bsz=bsz),
        grid=(ncb,),
        in_specs=[data(CT_BU), data(CT_BU + B_WIDTH), data(CT_BU + 2 * B_WIDTH), data(CT_BG),
                  cws(0), cws(1), cws(2),
                  pl.BlockSpec((2, wc), lambda j: (0, j)), pl.BlockSpec((2, wc), lambda j: (0, j)),
                  hs(0), hs(0), hs(1), hs(1),
                  pl.BlockSpec((2 * n1, n1), const), pl.BlockSpec((n1, 2 * n1), const),
                  pl.BlockSpec((2 * LANES, 2 * LANES), const), pl.BlockSpec((2 * LANES, 2 * LANES), const),
                  pl.BlockSpec((n1, LANES), const), pl.BlockSpec((n1, LANES), const)],
        out_specs=pl.BlockSpec((nh, wd), lambda j: (0, j)),
        out_shape=jax.ShapeDtypeStruct((nh, B_WIDTH * bsz * LANES), F32),
        scratch_shapes=[pltpu.VMEM((n1, ws), F32), pltpu.VMEM((n1, ws), F32), pltpu.VMEM((n1, ws), F32),
                        pltpu.VMEM((2 * n1, ws), F32), pltpu.VMEM((2 * n1, ws), F32)],
        compiler_params=_params("arbitrary"),
        name="hyena",
    )(pt2, pt2, pt2, pt2, cw, cw, cw, sk, scale_rep, hr, hi, hr, hi,
      jnp.asarray(tabs["f1m"], BF16), jnp.asarray(tabs["g1m"], BF16),
      jnp.asarray(tabs["f2m"], BF16), jnp.asarray(tabs["g2m"], BF16),
      jnp.asarray(tabs["twr"], F32), jnp.asarray(tabs["twi"], F32))


def _cumsum_rows(x):
    rows = x.shape[0]
    row = lax.broadcasted_iota(jnp.int32, x.shape, 0)
    k = 1
    while k < rows:
        x = x + jnp.where(row >= k, pltpu.roll(x, k, axis=0), 0.0)
        k *= 2
    return x


def _gla_chunk(q, z, v, lb, st_ref, o_ref, rows, forward):
    C = GLA_CHUNK
    half, quarter = C // 2, C // 4
    one_m = 1.0 - lb
    f = lb + one_m * _sigmoid(z)
    logf = jnp.log(jnp.maximum(f, F_FLOOR))
    kk = one_m * _sigmoid(-z)
    cum = _cumsum_rows(logf)
    tot = cum[C - 1:C]
    low = lax.broadcasted_iota(jnp.int32, (C, C_WIDTH), 0) < half
    if forward:
        acc = cum
        cross = acc[half - 1:half]
        leaf = jnp.where(low, acc[quarter - 1:quarter], acc[half + quarter - 1:half + quarter])
        q_side = jnp.logical_not(low)
    else:
        acc = tot - cum + logf
        cross = acc[half:half + 1]
        leaf = jnp.where(low, acc[quarter:quarter + 1], acc[half + quarter:half + quarter + 1])
        q_side = low
    e_leaf = acc - leaf
    ql = q * jnp.exp(e_leaf)
    kl = (kk * jnp.exp(-e_leaf)).astype(BF16)
    e_cross = acc - cross
    x_cross = jnp.exp(jnp.where(q_side, e_cross, -e_cross))
    qc = jnp.where(q_side, q * x_cross, 0.0)
    kc = jnp.where(q_side, 0.0, kk * x_cross).astype(BF16)
    qi = (q * jnp.exp(acc)).astype(BF16)
    ks = (kk * jnp.exp(tot - acc)).astype(BF16)
    dec = jnp.exp(tot)

    ti = lax.broadcasted_iota(jnp.int32, (2 * C, C), 0) % C
    si = lax.broadcasted_iota(jnp.int32, (2 * C, C), 1)
    causal = (ti >= si) if forward else (si >= ti)
    leaf_mask = jnp.logical_and(causal, (ti < half) == (si < half))
    first = lax.broadcasted_iota(jnp.int32, (C, LANES), 1) < C_HEAD_DIM
    r2 = lax.broadcasted_iota(jnp.int32, (LANES, LANES), 0) < C_HEAD_DIM
    c2 = lax.broadcasted_iota(jnp.int32, (LANES, LANES), 1) < C_HEAD_DIM
    same_head = r2 == c2

    def by_head(x):
        return jnp.concatenate([jnp.where(first, x, 0.0), jnp.where(first, 0.0, x)], axis=0).astype(BF16)

    for p in range(C_WIDTH // LANES):
        sl = slice(p * LANES, (p + 1) * LANES)
        vp = v[:, sl]
        s = (jnp.where(leaf_mask, _dot_nt(by_head(ql[:, sl]), kl[:, sl]), 0.0)
             + _dot_nt(by_head(qc[:, sl]), kc[:, sl])).astype(BF16)
        intra = _dot(jnp.concatenate([s[:C], s[C:]], axis=1), by_head(vp))
        st = st_ref[p]
        inter = _dot_nt(qi[:, sl], st.astype(BF16))
        o_ref[0, rows, sl] = intra + inter
        upd = _dot_tn(vp.astype(BF16), ks[:, sl])
        st_ref[p] = st * dec[:, sl] + jnp.where(same_head, upd, 0.0)


def _hgrn_kernel(qf_ref, zf_ref, vf_ref, qb_ref, zb_ref, vb_ref, lb_ref, of_ref, ob_ref, stf, stb, *, nsub):
    @pl.when(pl.program_id(1) == 0)
    def _():
        stf[...] = jnp.zeros_like(stf)
        stb[...] = jnp.zeros_like(stb)

    lbf, lbb = lb_ref[0:1, :], lb_ref[1:2, :]

    def body(j, carry):
        rf = pl.ds(pl.multiple_of(j * GLA_CHUNK, GLA_CHUNK), GLA_CHUNK)
        _gla_chunk(qf_ref[0, rf, :], zf_ref[0, rf, :], vf_ref[0, rf, :], lbf, stf, of_ref, rf, True)
        rb = pl.ds(pl.multiple_of((nsub - 1 - j) * GLA_CHUNK, GLA_CHUNK), GLA_CHUNK)
        _gla_chunk(qb_ref[0, rb, :], zb_ref[0, rb, :], vb_ref[0, rb, :], lbb, stb, ob_ref, rb, False)
        return carry

    lax.fori_loop(0, nsub, body, 0)


def _hgrn(c_all, lb, rb):
    bsz, L, _ = c_all.shape
    nblk = L // rb
    w = C_WIDTH
    fwd = lambda col: pl.BlockSpec((1, rb, w), lambda b, i: (b, i, col))
    bwd = lambda col: pl.BlockSpec((1, rb, w), lambda b, i: (b, nblk - 1 - i, col))
    npair = w // LANES
    return pl.pallas_call(
        functools.partial(_hgrn_kernel, nsub=rb // GLA_CHUNK),
        grid=(bsz, nblk),
        in_specs=[fwd(0), fwd(1), fwd(3), bwd(0), bwd(2), bwd(3), pl.BlockSpec((2, w), lambda b, i: (0, 0))],
        out_specs=[pl.BlockSpec((1, rb, w), lambda b, i: (b, i, 0)),
                   pl.BlockSpec((1, rb, w), lambda b, i: (b, nblk - 1 - i, 0))],
        out_shape=[jax.ShapeDtypeStruct((bsz, L, w), F32), jax.ShapeDtypeStruct((bsz, L, w), F32)],
        scratch_shapes=[pltpu.VMEM((npair, LANES, LANES), F32), pltpu.VMEM((npair, LANES, LANES), F32)],
        compiler_params=_params("arbitrary", "arbitrary"),
        name="hgrn",
    )(c_all, c_all, c_all, c_all, c_all, c_all, lb)


def _gelu_tanh(x):
    return 0.5 * x * (1.0 + jnp.tanh(math.sqrt(2.0 / math.pi) * (x + 0.044715 * (x * x * x))))


def _out_kernel(x_ref, mod_ref, ys_ref, au_ref, ag_ref, hy_ref, of_ref, ob_ref, cg_ref, d_ref, gwt_ref, gb_ref,
                hn_ref, ones_ref, wo_ref, fn_ref, o_ref, *, final):
    o = of_ref[0] + ob_ref[0]
    o2 = o * o
    hi, lo = _split_bf16(o2)
    ms = (_dot(hi, ones_ref[...]) + _dot(lo, ones_ref[...])) * (1.0 / C_HEAD_DIM)
    c_out = (o * lax.rsqrt(ms + EPS) * hn_ref[...] * _silu(cg_ref[0])).astype(BF16)
    mixed_c = _dot(c_out, wo_ref[A_WIDTH + B_WIDTH:, :])
    gate = mod_ref[0, 2:3, :]
    for j in range(ys_ref.shape[0]):
        y = ys_ref[j] + au_ref[j] * d_ref[...]
        z = _gelu_tanh(y)
        glu = _sigmoid(_dot(gwt_ref[...], z.astype(BF16)) + gb_ref[...])
        a_out = (z * glu * _silu(ag_ref[j])).astype(BF16)
        mixed = (_dot_tn(a_out, wo_ref[:A_WIDTH, :]) + _dot_tn(hy_ref[j].astype(BF16), wo_ref[A_WIDTH:A_WIDTH + B_WIDTH, :])
                 + mixed_c[j * LANES:(j + 1) * LANES, :])
        xn = x_ref[0, j * LANES:(j + 1) * LANES, :] + gate * mixed
        if final:
            msq = jnp.mean(xn * xn, axis=-1, keepdims=True)
            xn = xn * lax.rsqrt(msq + EPS) * fn_ref[...]
        o_ref[0, j * LANES:(j + 1) * LANES, :] = xn


def _out(x, mod, ys3, pt3, hy3, of, ob, c_all, d_rep, gwt, gb_rep, hn, ones_bd, wo, fn, tb, final):
    bsz, L, dm = x.shape
    nt = tb // LANES
    row = lambda w, col: pl.BlockSpec((1, tb, w), lambda b, i: (b, i, col))
    ct = lambda ch, blk: pl.BlockSpec((nt, ch, LANES), lambda b, i: (i, blk, b))
    const = lambda b, i: (0, 0)
    full = lambda a: pl.BlockSpec(a.shape, const)
    return pl.pallas_call(
        functools.partial(_out_kernel, final=final),
        grid=(bsz, L // tb),
        in_specs=[row(dm, 0), pl.BlockSpec((1, 3, dm), lambda b, i: (b, 0, 0)),
                  ct(A_WIDTH, 0), ct(A_WIDTH, CT_AU // A_WIDTH), ct(A_WIDTH, CT_AG // A_WIDTH), ct(B_WIDTH, 0),
                  row(C_WIDTH, 0), row(C_WIDTH, 0), row(C_WIDTH, 4),
                  full(d_rep), full(gwt), full(gb_rep), full(hn), full(ones_bd), full(wo), full(fn)],
        out_specs=row(dm, 0),
        out_shape=jax.ShapeDtypeStruct((bsz, L, dm), F32),
        compiler_params=_params("arbitrary", "arbitrary"),
        name="out",
    )(x, mod, ys3, pt3, pt3, hy3, of, ob, c_all, d_rep, gwt, gb_rep, hn, ones_bd, wo, fn)


def _trunk(x, ada_rows, p, tb=256, hg_rb=256):
    bsz, L, _ = x.shape
    nchunk = L // LANES
    head = np.arange(C_WIDTH) // C_HEAD_DIM
    ones_bd = jnp.asarray(head[:, None] == head[None, :], BF16)
    for layer in range(DEPTH):
        mod = ada_rows[layer].reshape(bsz, 3, D_MODEL)
        w_in = p["w_in"][layer]
        o_a, o_b = 2 * A_WIDTH, 2 * A_WIDTH + 4 * B_WIDTH
        wt = w_in[:, :o_b].T.astype(BF16)
        wc = w_in[:, o_b:].astype(BF16)
        pt, c_all = _proj(x, mod, p["norm_w"][layer].reshape(1, -1).astype(F32), wt, wc)

        ys4 = _s5(pt.reshape(nchunk, CT_ROWS, bsz, LANES), p["s5_consts"][layer])
        hy2 = _hyena(pt.reshape(nchunk, CT_ROWS * bsz * LANES), p["hy_filters"][(layer, L)], p["hy_conv_w"][layer],
                     p["hy_conv_b"][layer], p["hy_skip"][layer], L, bsz)
        of, ob = _hgrn(c_all, p["lower_bounds"][layer], hg_rb)

        rep = lambda v: jnp.repeat(v.astype(F32).reshape(-1, 1), LANES, axis=1)
        x = _out(x, mod, ys4.reshape(nchunk, A_WIDTH, bsz * LANES), pt, hy2.reshape(nchunk, B_WIDTH, bsz * LANES),
                 of, ob, c_all, rep(p["s5_d"][layer]), p["s5_glu_w"][layer].T.astype(BF16), rep(p["s5_glu_b"][layer]),
                 jnp.tile(p["hg_norm_w"][layer].astype(F32), C_WIDTH // C_HEAD_DIM).reshape(1, -1),
                 ones_bd, p["w_out"][layer].astype(BF16), p["final_norm_w"].reshape(1, -1).astype(F32),
                 tb, layer == DEPTH - 1)
    return x


def kernel(x_prompt, x_sample, c_prompt, c_sample, norm_w, ada_w, ada_b, w_in, w_out, s5_lambda_re, s5_lambda_im, s5_log_dt, s5_b_re, s5_b_im, s5_c_re, s5_c_im, s5_d, s5_glu_w, s5_glu_b, hy_conv_w, hy_conv_b, hy_w1, hy_b1, hy_w2, hy_b2, hy_w3, hy_sin_freq, hy_log_decay, hy_skip, hg_lb_logits, hg_norm_w, final_norm_w):
    bp = x_prompt.shape[0]
    ada = _ada(jnp.concatenate([c_prompt, c_sample], axis=0).astype(F32), ada_w.astype(F32), ada_b.astype(F32))
    lb_soft = jax.nn.softmax(hg_lb_logits.astype(F32), axis=0)
    lower_bounds = jnp.cumsum(lb_soft, axis=0) - lb_soft[0:1]
    lengths = sorted({x_prompt.shape[1], x_sample.shape[1]})
    p = dict(
        norm_w=norm_w, w_in=w_in, w_out=w_out, s5_d=s5_d, s5_glu_w=s5_glu_w, s5_glu_b=s5_glu_b,
        hy_conv_w=hy_conv_w, hy_conv_b=hy_conv_b, hy_skip=hy_skip, hg_norm_w=hg_norm_w, final_norm_w=final_norm_w,
        lower_bounds=lower_bounds,
        s5_consts=[_s5_consts(s5_lambda_re[l], s5_lambda_im[l], s5_log_dt[l], s5_b_re[l], s5_b_im[l],
                              s5_c_re[l], s5_c_im[l]) for l in range(DEPTH)],
        hy_filters={(l, L): _hyena_filters(L, hy_w1[l], hy_b1[l], hy_w2[l], hy_b2[l], hy_w3[l], hy_sin_freq[l],
                                           hy_log_decay[l])
                    for l in range(DEPTH) for L in lengths},
    )
    y_prompt = _trunk(x_prompt, ada[:, :bp], p)
    y_sample = _trunk(x_sample, ada[:, bp:], p)
    return (y_prompt, y_sample)
```

```python
import functools
import math

import numpy as np
import jax
import jax.numpy as jnp
from jax import lax
from jax.experimental import pallas as pl
from jax.experimental.pallas import tpu as pltpu

F32 = jnp.float32
BF16 = jnp.bfloat16
HIGHEST = lax.Precision.HIGHEST

D_MODEL = 1024
DEPTH = 2
A_WIDTH = 256
A_GROUP_CH = 16
A_GROUPS = 16
A_STATE = 64
B_WIDTH = 384
HY_ORDER = 2
HY_BANDS = 16
HY_POS_FEAT = 1 + 2 * HY_BANDS
HY_FILT_HID = 64
C_WIDTH = 384
C_HEAD_DIM = 64
EPS = 1e-6
F_FLOOR = 1e-30

LANES = 128
SUBLANES = 8
GLA_CHUNK = 128
VMEM_LIMIT = 56 * 1024 * 1024

CT_AG, CT_BU, CT_BG = 0, A_WIDTH, A_WIDTH + 3 * B_WIDTH
CT_ROWS = A_WIDTH + 4 * B_WIDTH
NAT_COLS = 5 * C_WIDTH


def _dot(a, b, precision=None):
    return jnp.dot(a, b, preferred_element_type=F32, precision=precision)


def _dot_nt(a, b, precision=None):
    return lax.dot_general(a, b, (((1,), (1,)), ((), ())), preferred_element_type=F32, precision=precision)


def _dot_tn(a, b, precision=None):
    return lax.dot_general(a, b, (((0,), (0,)), ((), ())), preferred_element_type=F32, precision=precision)


def _split_bf16(a):
    hi = a.astype(BF16)
    return hi, (a - hi.astype(F32)).astype(BF16)


def _sigmoid(x):
    return 1.0 / (1.0 + jnp.exp(-x))


def _silu(x):
    return x * _sigmoid(x)


def _params(*sem):
    return pltpu.CompilerParams(dimension_semantics=sem, vmem_limit_bytes=VMEM_LIMIT)


def _resident(shape, index_map):
    return pl.BlockSpec(shape, index_map, pipeline_mode=pl.Buffered(1))


def _ada_kernel(c_ref, w_ref, b_ref, o_ref):
    cond = _silu(c_ref[...])
    o_ref[0] = _dot(cond, w_ref[0], precision=HIGHEST) + b_ref[0]


def _ada(c_all, ada_w, ada_b):
    nb = c_all.shape[0]
    d = D_MODEL
    return pl.pallas_call(
        _ada_kernel,
        grid=(DEPTH, 3),
        in_specs=[pl.BlockSpec((nb, d), lambda l, j: (0, 0)),
                  pl.BlockSpec((1, d, d), lambda l, j: (l, 0, j)),
                  pl.BlockSpec((1, 1, d), lambda l, j: (l, 0, j))],
        out_specs=pl.BlockSpec((1, nb, d), lambda l, j: (l, 0, j)),
        out_shape=jax.ShapeDtypeStruct((DEPTH, nb, 3 * d), F32),
        compiler_params=_params("arbitrary", "arbitrary"),
        name="ada",
    )(c_all, ada_w, ada_b.reshape(DEPTH, 1, 3 * d))


def _proj_kernel(x_ref, mod_ref, nw_ref, wt_ref, wc_ref, au_ref, pt_ref, c_ref):
    bsz = x_ref.shape[0]
    nb = 2
    rows = 256
    for s in range(bsz // nb):
        bs = slice(s * nb, (s + 1) * nb)
        x = x_ref[bs]
        ms = jnp.mean(x * x, axis=-1, keepdims=True)
        h = x * lax.rsqrt(ms + EPS) * nw_ref[...]
        h = h * (1.0 + mod_ref[bs, 1:2, :]) + mod_ref[bs, 0:1, :]
        hb = h.astype(BF16).reshape(nb * LANES, D_MODEL)
        ls = slice(s * nb * LANES, (s + 1) * nb * LANES)
        au_ref[0, :, ls] = _dot_nt(wt_ref[0:A_WIDTH, :], hb)
        for r in range(0, CT_ROWS, rows):
            pt_ref[0, r:r + rows, ls] = _dot_nt(wt_ref[A_WIDTH + r:A_WIDTH + r + rows, :], hb).astype(BF16)
        for k in range(NAT_COLS // C_WIDTH):
            sl = slice(k * C_WIDTH, (k + 1) * C_WIDTH)
            c_ref[bs, :, sl] = _dot(hb, wc_ref[:, sl]).astype(BF16).reshape(nb, LANES, C_WIDTH)


def _proj(x, mod, norm_w, wt, wc):
    bsz, L, d = x.shape
    nchunk = L // LANES
    return pl.pallas_call(
        _proj_kernel,
        grid=(nchunk,),
        in_specs=[pl.BlockSpec((bsz, LANES, d), lambda i: (0, i, 0)),
                  _resident((bsz, 3, d), lambda i: (0, 0, 0)),
                  _resident((1, d), lambda i: (0, 0)),
                  _resident(wt.shape, lambda i: (0, 0)),
                  _resident(wc.shape, lambda i: (0, 0))],
        out_specs=[pl.BlockSpec((1, A_WIDTH, bsz * LANES), lambda i: (i, 0, 0)),
                   pl.BlockSpec((1, CT_ROWS, bsz * LANES), lambda i: (i, 0, 0)),
                   pl.BlockSpec((bsz, LANES, NAT_COLS), lambda i: (0, i, 0))],
        out_shape=[jax.ShapeDtypeStruct((nchunk, A_WIDTH, bsz * LANES), F32),
                   jax.ShapeDtypeStruct((nchunk, CT_ROWS, bsz * LANES), BF16),
                   jax.ShapeDtypeStruct((bsz, L, NAT_COLS), BF16)],
        compiler_params=_params("arbitrary"),
        name="proj",
    )(x, mod, norm_w, wt, wc)


def _s5_consts(lam_re, lam_im, log_dt, b_re, b_im, c_re, c_im):
    T = LANES
    G, P, H = A_GROUPS, A_STATE, A_GROUP_CH
    dt = jnp.exp(log_dt.astype(F32))[..., None]
    lr, li = lam_re.astype(F32) * dt, lam_im.astype(F32) * dt

    def power(k):
        mag = jnp.exp(lr * k)
        return mag * jnp.cos(li * k), mag * jnp.sin(li * k)

    def cmul(ar, ai, br, bi):
        return ar * br - ai * bi, ar * bi + ai * br

    p1r, p1i = power(1.0)
    den = lam_re.astype(F32) ** 2 + lam_im.astype(F32) ** 2
    cfr, cfi = cmul(p1r - 1.0, p1i, lam_re.astype(F32) / den, -lam_im.astype(F32) / den)
    bbr, bbi = cmul(cfr[..., None], cfi[..., None], b_re.astype(F32)[None], b_im.astype(F32)[None])
    cr, ci = c_re.astype(F32), c_im.astype(F32)

    k = jnp.arange(T + 1, dtype=F32)[:, None, None, None]
    pwr, pwi = power(k)
    cpr, cpi = cmul(cr[None], ci[None], pwr[:T, :, :, None, :], pwi[:T, :, :, None, :])
    kern = (jnp.einsum("kdghp,dgpi->dkghi", cpr, bbr, precision=HIGHEST)
            - jnp.einsum("kdghp,dgpi->dkghi", cpi, bbi, precision=HIGHEST))
    two = jnp.concatenate([kern[1][:0:-1], (kern[0][0] + kern[1][0])[None], kern[0][1:],
                           jnp.zeros((1,) + kern.shape[2:], F32)], axis=0)
    k2 = two.transpose(1, 3, 2, 0).reshape(G, H * H, 2 * T)

    sf_r, sf_i = cmul(pwr[:T][::-1, 0, :, :, None], pwi[:T][::-1, 0, :, :, None], bbr[0][None], bbi[0][None])
    sb_r, sb_i = cmul(pwr[:T, 1, :, :, None], pwi[:T, 1, :, :, None], bbr[1][None], bbi[1][None])
    bmat = jnp.concatenate([sf_r, sb_r, sf_i, sb_i], axis=2)
    bmat = bmat.transpose(1, 3, 0, 2).reshape(G, H * T, 4 * P)

    mf_r, mf_i = cmul(cr[0][None], ci[0][None], pwr[1:, 0, :, None, :], pwi[1:, 0, :, None, :])
    mb_r, mb_i = cmul(cr[1][None], ci[1][None], pwr[1:][::-1, 1, :, None, :], pwi[1:][::-1, 1, :, None, :])
    zero = jnp.zeros_like(mf_r)
    cf = jnp.concatenate([mf_r, zero, -mf_i, zero], axis=3)
    cb = jnp.concatenate([zero, mb_r, zero, -mb_i], axis=3)
    cf = cf.transpose(1, 3, 2, 0).reshape(G, 4 * P, H * T)
    cb = cb.transpose(1, 3, 2, 0).reshape(G, 4 * P, H * T)

    lam_t = jnp.stack([jnp.concatenate([pwr[T, 0], pwr[T, 1]], axis=-1),
                       jnp.concatenate([pwi[T, 0], pwi[T, 1]], axis=-1)], axis=1)
    return k2, bmat.astype(BF16), cf.astype(BF16), cb.astype(BF16), lam_t


def _s5_kernel(u_ref, k2_ref, b_ref, cf_ref, cb_ref, lam_ref, y_ref, tt, a_scr, sr_scr, si_scr, xf_scr, xb_scr):
    nchunk = u_ref.shape[0]
    m = nchunk * SUBLANES
    H, T = A_GROUP_CH, LANES

    def toeplitz(hp, carry):
        rows = k2_ref[0, pl.ds(pl.multiple_of(hp * H, H), H), :]
        for h in range(H):
            x = jnp.broadcast_to(rows[h:h + 1, :], (T, 2 * T))
            r = pltpu.roll(x, 1, axis=1, stride=1, stride_axis=0)
            tt[pl.ds(pl.multiple_of(hp * T, T), T), h * T:(h + 1) * T] = r[:, T:].astype(BF16)
        return carry

    lax.fori_loop(0, H, toeplitz, 0)

    for hp in range(H):
        a_scr[:, hp * T:(hp + 1) * T] = u_ref[:, hp].reshape(m, T).astype(BF16)
    s = _dot(a_scr[...], b_ref[0])
    sr_scr[...] = s[:, :LANES]
    si_scr[...] = s[:, LANES:]

    lam_r = jnp.broadcast_to(lam_ref[0, 0:1, :], (SUBLANES, LANES))
    lam_i = jnp.broadcast_to(lam_ref[0, 1:2, :], (SUBLANES, LANES))
    fwd_lane = lax.broadcasted_iota(jnp.int32, (SUBLANES, LANES), 1) < A_STATE

    def step(i, carry):
        xr, xi = carry
        rf = pl.ds(pl.multiple_of(i * SUBLANES, SUBLANES), SUBLANES)
        rb = pl.ds(pl.multiple_of((nchunk - 1 - i) * SUBLANES, SUBLANES), SUBLANES)
        xf_scr[rf, 0:LANES] = xr
        xf_scr[rf, LANES:2 * LANES] = xi
        xb_scr[rb, 0:LANES] = xr
        xb_scr[rb, LANES:2 * LANES] = xi
        sr = jnp.where(fwd_lane, sr_scr[rf, :], sr_scr[rb, :])
        si = jnp.where(fwd_lane, si_scr[rf, :], si_scr[rb, :])
        return lam_r * xr - lam_i * xi + sr, lam_r * xi + lam_i * xr + si

    zero = jnp.zeros((SUBLANES, LANES), F32)
    lax.fori_loop(0, nchunk, step, (zero, zero))

    a = a_scr[...]
    xf = xf_scr[...].astype(BF16)
    xb = xb_scr[...].astype(BF16)
    for j in range(H // 2):
        sl = slice(j * 2 * T, (j + 1) * 2 * T)
        yj = _dot(a, tt[:, sl]) + _dot(xf, cf_ref[0, :, sl]) + _dot(xb, cb_ref[0, :, sl])
        y_ref[:, 2 * j] = yj[:, :T].reshape(nchunk, SUBLANES, T)
        y_ref[:, 2 * j + 1] = yj[:, T:].reshape(nchunk, SUBLANES, T)


def _s5(au4, consts):
    k2, bmat, cf, cb, lam_t = consts
    nchunk, _, bsz, _ = au4.shape
    assert bsz == SUBLANES
    G, H, T = A_GROUPS, A_GROUP_CH, LANES
    m = nchunk * bsz
    per_g = lambda g: (g, 0, 0)
    return pl.pallas_call(
        _s5_kernel,
        grid=(G,),
        in_specs=[pl.BlockSpec((nchunk, H, bsz, T), lambda g: (0, g, 0, 0)),
                  pl.BlockSpec((1, H * H, 2 * T), per_g),
                  pl.BlockSpec((1, H * T, 4 * A_STATE), per_g),
                  pl.BlockSpec((1, 4 * A_STATE, H * T), per_g),
                  pl.BlockSpec((1, 4 * A_STATE, H * T), per_g),
                  pl.BlockSpec((1, 2, LANES), per_g)],
        out_specs=pl.BlockSpec((nchunk, H, bsz, T), lambda g: (0, g, 0, 0)),
        out_shape=jax.ShapeDtypeStruct((nchunk, A_WIDTH, bsz, T), F32),
        scratch_shapes=[pltpu.VMEM((H * T, H * T), BF16), pltpu.VMEM((m, H * T), BF16),
                        pltpu.VMEM((m, LANES), F32), pltpu.VMEM((m, LANES), F32),
                        pltpu.VMEM((m, 2 * LANES), F32), pltpu.VMEM((m, 2 * LANES), F32)],
        compiler_params=_params("arbitrary"),
        name="s5",
    )(au4, k2, bmat, cf, cb, lam_t)


def _dft_tables(L):
    N = 2 * L
    n2 = LANES
    n1 = N // n2
    h = n1 // 2
    k = np.arange(n1)
    f1 = np.exp(-2j * np.pi * np.outer(k, k) / n1)
    f2 = np.exp(-2j * np.pi * np.outer(np.arange(n2), np.arange(n2)) / n2)
    tw = np.exp(-2j * np.pi * np.outer(k, np.arange(n2)) / N)
    f1m = np.block([[f1.real[:, :h], -f1.imag[:, :h]], [f1.imag[:, :h], f1.real[:, :h]]])
    ffm = np.concatenate([f1.real, f1.imag], axis=0)
    f2m = np.block([[f2.real, f2.imag], [-f2.imag, f2.real]])
    g2m = np.block([[f2.real, -f2.imag], [f2.imag, f2.real]])
    g1 = np.conj(f1)[:h, :]
    g1m = np.block([[g1.real, -g1.imag], [g1.imag, g1.real]])
    return dict(n1=n1, f1m=f1m, ffm=ffm, f2m=f2m, g2m=g2m, g1m=g1m, twr=tw.real, twi=tw.imag)


def _filter_positions(L):
    N = 2 * L
    i = np.arange(N)
    pos = np.where(i < L, i, N - i).astype(np.float32)
    t = pos / np.float32(max(L - 1, 1))
    w = np.float32(2.0 * math.pi) * pos / np.float32(L)
    bands = np.linspace(1e-4, HY_BANDS - 1, HY_BANDS, dtype=np.float32)
    wb = (w[:, None] * bands).astype(np.float32)
    feats = np.concatenate([t[:, None], np.cos(wb), -np.sin(wb)], axis=-1).astype(np.float32)
    fpad = np.zeros((LANES, N), np.float32)
    fpad[:HY_POS_FEAT] = feats.T
    keep = np.ones((N,), np.float32)
    keep[L] = 0.0
    n1 = N // LANES
    return fpad, t.reshape(n1, 1, LANES), keep.reshape(n1, 1, LANES)


def _filt_kernel(feat_ref, w1t_ref, b1_ref, w2t_ref, b2_ref, w3t_ref, sf_ref, ld_ref, t_ref, keep_ref,
                 ts_ref, scale_ref, ss_scr, dec_scr, *, inv_n, half):
    i = pl.program_id(0)

    @pl.when(i % half == 0)
    def _():
        dec_scr[...] = jnp.exp(ld_ref[0])

    @pl.when(i == 0)
    def _():
        ss_scr[...] = jnp.zeros_like(ss_scr)

    h1 = jnp.sin(sf_ref[:, 0:1] * (_dot(w1t_ref[...], feat_ref[...], precision=HIGHEST) + b1_ref[...]))
    h2 = jnp.sin(sf_ref[:, 1:2] * (_dot(w2t_ref[...], h1, precision=HIGHEST) + b2_ref[...]))
    w_hi, w_lo = _split_bf16(w3t_ref[0])
    h_hi, h_lo = _split_bf16(h2)
    h3t = _dot(w_hi, h_hi) + _dot(w_hi, h_lo) + _dot(w_lo, h_hi)
    win = jnp.exp(-dec_scr[...] * t_ref[0])
    ts = h3t * win * keep_ref[0]
    ts_ref[0] = ts.astype(BF16)
    ss_scr[...] += ts * ts

    @pl.when(i == pl.num_programs(0) - 1)
    def _():
        tot = jnp.sum(ss_scr[...], axis=-1, keepdims=True)
        scale_ref[...] = lax.rsqrt(tot + EPS) * inv_n


def _filt_fft_kernel(ts_ref, ffm_ref, twr_ref, twi_ref, f2m_ref, hr_ref, hi_ref, *, n1, cb):
    tr, ti = twr_ref[...], twi_ref[...]
    group = 8
    for g in range(cb // group):
        a = _dot(ffm_ref[...], ts_ref[:, g * group * LANES:(g + 1) * group * LANES])
        for c in range(group):
            sl = slice(c * LANES, (c + 1) * LANES)
            ar, ai = a[:n1, sl], a[n1:, sl]
            cat = jnp.concatenate([ar * tr - ai * ti, ar * ti + ai * tr], axis=1).astype(BF16)
            x = _dot(cat, f2m_ref[...]).astype(BF16)
            out = slice((g * group + c) * LANES, (g * group + c + 1) * LANES)
            hr_ref[:, out] = x[:, :LANES]
            hi_ref[:, out] = x[:, LANES:]


def _hyena_filters(L, w1, b1, w2, b2, w3, sin_freq, log_decay, cb=32):
    tabs = _dft_tables(L)
    n1 = tabs["n1"]
    N = 2 * L
    nch = HY_ORDER * B_WIDTH
    fpad, tpos, keep = _filter_positions(L)
    w1t = jnp.zeros((HY_FILT_HID, LANES), F32).at[:, :HY_POS_FEAT].set(w1.astype(F32).T)
    w3t = w3.astype(F32).T.reshape(2, nch, HY_FILT_HID)
    ld = log_decay.astype(F32).reshape(2, nch, 1)
    half = n1 // 2
    col = lambda v: v.astype(F32).reshape(-1, 1)
    fixed = lambda shape: pl.BlockSpec(shape, lambda i: (0,) * len(shape))
    ts, scale = pl.pallas_call(
        functools.partial(_filt_kernel, inv_n=1.0 / N, half=half),
        grid=(n1,),
        in_specs=[pl.BlockSpec((LANES, LANES), lambda i: (0, i)),
                  fixed((HY_FILT_HID, LANES)), fixed((HY_FILT_HID, 1)),
                  fixed((HY_FILT_HID, HY_FILT_HID)), fixed((HY_FILT_HID, 1)),
                  pl.BlockSpec((1, nch, HY_FILT_HID), lambda i: (i // half, 0, 0)),
                  fixed((HY_FILT_HID, 2)),
                  pl.BlockSpec((1, nch, 1), lambda i: (i // half, 0, 0)),
                  pl.BlockSpec((1, 1, LANES), lambda i: (i, 0, 0)),
                  pl.BlockSpec((1, 1, LANES), lambda i: (i, 0, 0))],
        out_specs=[pl.BlockSpec((1, nch, LANES), lambda i: (i, 0, 0)),
                   pl.BlockSpec((nch, 1), lambda i: (0, 0))],
        out_shape=[jax.ShapeDtypeStruct((n1, nch, LANES), BF16),
                   jax.ShapeDtypeStruct((nch, 1), F32)],
        scratch_shapes=[pltpu.VMEM((nch, LANES), F32), pltpu.VMEM((nch, 1), F32)],
        compiler_params=_params("arbitrary"),
        name="hy_filter",
    )(jnp.asarray(fpad), w1t, col(b1), w2.astype(F32).T, col(b2), w3t, sin_freq.astype(F32).T, ld,
      jnp.asarray(tpos), jnp.asarray(keep))
    W = cb * LANES
    hr, hi = pl.pallas_call(
        functools.partial(_filt_fft_kernel, n1=n1, cb=cb),
        grid=(nch // cb,),
        in_specs=[pl.BlockSpec((n1, W), lambda j: (0, j)),
                  pl.BlockSpec((2 * n1, n1), lambda j: (0, 0)),
                  pl.BlockSpec((n1, LANES), lambda j: (0, 0)),
                  pl.BlockSpec((n1, LANES), lambda j: (0, 0)),
                  pl.BlockSpec((2 * LANES, 2 * LANES), lambda j: (0, 0))],
        out_specs=[pl.BlockSpec((n1, W), lambda j: (0, j)),
                   pl.BlockSpec((n1, W), lambda j: (0, j))],
        out_shape=[jax.ShapeDtypeStruct((n1, nch * LANES), BF16),
                   jax.ShapeDtypeStruct((n1, nch * LANES), BF16)],
        compiler_params=_params("arbitrary"),
        name="hy_filter_fft",
    )(ts.reshape(n1, nch * LANES), jnp.asarray(tabs["ffm"], F32).astype(BF16), jnp.asarray(tabs["twr"], F32),
      jnp.asarray(tabs["twi"], F32), jnp.asarray(tabs["f2m"], F32).astype(BF16))
    scale_rep = jnp.repeat(scale.reshape(HY_ORDER, B_WIDTH), LANES, axis=1)
    return hr, hi, scale_rep


def _shift_time(x, direction):
    rows = x.shape[0]
    lane = lax.broadcasted_iota(jnp.int32, x.shape, 1)
    row = lax.broadcasted_iota(jnp.int32, x.shape, 0)
    top = row[:SUBLANES]
    if direction > 0:
        r = pltpu.roll(x, 1, axis=1)
        e = pltpu.roll(r, 1, axis=0)
        edge = jnp.concatenate([jnp.where(top == 0, 0.0, e[:SUBLANES]), e[SUBLANES:]], axis=0)
        return jnp.where(lane == 0, edge, r)
    r = pltpu.roll(x, LANES - 1, axis=1)
    e = pltpu.roll(r, rows - 1, axis=0)
    edge = jnp.concatenate([e[:rows - SUBLANES], jnp.where(top == SUBLANES - 1, 0.0, e[rows - SUBLANES:])], axis=0)
    return jnp.where(lane == LANES - 1, edge, r)


def _hyena_kernel(g0_ref, g1_ref, z_ref, bg_ref, cw0_ref, cw1_ref, cwz_ref, skip_ref, scale_ref,
                  h0r_ref, h0i_ref, h1r_ref, h1i_ref, f1m_ref, g1m_ref, f2m_ref, g2m_ref, twr_ref, twi_ref,
                  o_ref, z_scr, g0_scr, g1_scr, a_scr, b_scr, *, n1, cb, bsz):
    nh = n1 // 2
    npair = bsz // 2
    tile = lambda i: slice(i * LANES, (i + 1) * LANES)

    def short_conv(src, cw_ref, dst):
        for c in range(cb):
            w0, w1, w2, bias = (cw_ref[k:k + 1, tile(c)] for k in range(4))
            for b in range(bsz):
                x = src[:, tile(c * bsz + b)].astype(F32)
                y = bias + w0 * _shift_time(x, 1) + w1 * x + w2 * _shift_time(x, -1)
                half, p = divmod(b, npair)
                dst[half * nh:(half + 1) * nh, tile(c * npair + p)] = y

    short_conv(g0_ref, cw0_ref, g0_scr)
    short_conv(g1_ref, cw1_ref, g1_scr)
    short_conv(z_ref, cwz_ref, z_scr)
    tr, ti = twr_ref[...].astype(BF16), twi_ref[...].astype(BF16)

    def long_conv(hr_ref, hi_ref):
        a_scr[...] = _dot(f1m_ref[...], z_scr[...].astype(BF16)).astype(BF16)
        for c in range(cb):
            hr, hi = hr_ref[:, tile(c)], hi_ref[:, tile(c)]
            for p in range(npair):
                sl = tile(c * npair + p)
                ar, ai = a_scr[0:n1, sl], a_scr[n1:2 * n1, sl]
                x = _dot(jnp.concatenate([ar * tr - ai * ti, ar * ti + ai * tr], axis=1), f2m_ref[...]).astype(BF16)
                xr, xi = x[:, :LANES], x[:, LANES:]
                y = _dot(jnp.concatenate([xr * hr - xi * hi, xr * hi + xi * hr], axis=1), g2m_ref[...]).astype(BF16)
                br, bi = y[:, :LANES], y[:, LANES:]
                b_scr[0:n1, sl] = br * tr + bi * ti
                b_scr[n1:2 * n1, sl] = bi * tr - br * ti
        return _dot(g1m_ref[...], b_scr[...])

    def per_channel(ref, row):
        return jnp.concatenate([ref[row:row + 1, tile(c)] for c in range(cb) for _ in range(npair)], axis=1)

    z0 = z_scr[...]
    z1 = g0_scr[...] * (long_conv(h0r_ref, h0i_ref) * per_channel(scale_ref, 0) + z0 * per_channel(skip_ref, 0))
    z_scr[...] = z1
    z2 = g1_scr[...] * (long_conv(h1r_ref, h1i_ref) * per_channel(scale_ref, 1) + z1 * per_channel(skip_ref, 1))
    z_scr[...] = z2
    for c in range(cb):
        for b in range(bsz):
            half, p = divmod(b, npair)
            gate = _silu(bg_ref[:, tile(c * bsz + b)].astype(F32))
            o_ref[:, tile(c * bsz + b)] = (z_scr[half * nh:(half + 1) * nh, tile(c * npair + p)] * gate).astype(BF16)


def _hyena(pt2, filt, conv_w, conv_b, skip, L, bsz, cb=2):
    hr, hi, scale_rep = filt
    tabs = _dft_tables(L)
    n1 = tabs["n1"]
    nh = n1 // 2
    assert nh % SUBLANES == 0, "sequence length must be a multiple of 1024"
    wd = cb * bsz * LANES
    wc = cb * LANES
    ws = cb * (bsz // 2) * LANES
    ncb = B_WIDTH // cb
    cw = jnp.repeat(jnp.concatenate([conv_w.astype(F32), conv_b.astype(F32)[None]], axis=0), LANES, axis=1)
    sk = jnp.repeat(skip.astype(F32), LANES, axis=1)
    data = lambda row0: pl.BlockSpec((nh, wd), lambda j: (0, row0 // cb + j))
    cws = lambda grp: pl.BlockSpec((4, wc), lambda j: (0, grp * ncb + j))
    hs = lambda order: pl.BlockSpec((n1, wc), lambda j: (0, order * ncb + j))
    const = lambda j: (0, 0)
    return pl.pallas_call(
        functools.partial(_hyena_kernel, n1=n1, cb=cb, bsz=bsz),
        grid=(ncb,),
        in_specs=[data(CT_BU), data(CT_BU + B_WIDTH), data(CT_BU + 2 * B_WIDTH), data(CT_BG),
                  cws(0), cws(1), cws(2),
                  pl.BlockSpec((2, wc), lambda j: (0, j)), pl.BlockSpec((2, wc), lambda j: (0, j)),
                  hs(0), hs(0), hs(1), hs(1),
                  pl.BlockSpec((2 * n1, n1), const), pl.BlockSpec((n1, 2 * n1), const),
                  pl.BlockSpec((2 * LANES, 2 * LANES), const), pl.BlockSpec((2 * LANES, 2 * LANES), const),
                  pl.BlockSpec((n1, LANES), const), pl.BlockSpec((n1, LANES), const)],
        out_specs=pl.BlockSpec((nh, wd), lambda j: (0, j)),
        out_shape=jax.ShapeDtypeStruct((nh, B_WIDTH * bsz * LANES), BF16),
        scratch_shapes=[pltpu.VMEM((n1, ws), F32), pltpu.VMEM((n1, ws), F32), pltpu.VMEM((n1, ws), F32),
                        pltpu.VMEM((2 * n1, ws), BF16), pltpu.VMEM((2 * n1, ws), BF16)],
        compiler_params=_params("arbitrary"),
        name="hyena",
    )(pt2, pt2, pt2, pt2, cw, cw, cw, sk, scale_rep, hr, hi, hr, hi,
      jnp.asarray(tabs["f1m"], F32).astype(BF16), jnp.asarray(tabs["g1m"], F32).astype(BF16),
      jnp.asarray(tabs["f2m"], F32).astype(BF16), jnp.asarray(tabs["g2m"], F32).astype(BF16),
      jnp.asarray(tabs["twr"], F32), jnp.asarray(tabs["twi"], F32))


def _gla_chunk(q, z, v, lb, tri, st_ref, o_ref, rows, forward):
    C = GLA_CHUNK
    half, quarter = C // 2, C // 4
    one_m = 1.0 - lb
    sig = _sigmoid(z)
    f = lb + one_m * sig
    logf = jnp.log2(jnp.maximum(f, F_FLOOR))
    kk = one_m * (1.0 - sig)
    l_hi, l_lo = _split_bf16(logf)
    acc = _dot(tri, l_hi) + _dot(tri, l_lo)
    if forward:
        tot = acc[C - 1:C]
        a_low, a_high, a_cross = acc[quarter - 1:quarter], acc[half + quarter - 1:half + quarter], acc[half - 1:half]
    else:
        tot = acc[0:1]
        a_low, a_high, a_cross = acc[quarter:quarter + 1], acc[half + quarter:half + quarter + 1], acc[half:half + 1]
    e_leaf = jnp.concatenate([acc[:half] - a_low, acc[half:] - a_high], axis=0)
    ql = (q * jnp.exp2(e_leaf)).astype(BF16)
    kl = (kk * jnp.exp2(-e_leaf)).astype(BF16)
    zeros = jnp.zeros((half, C_WIDTH), BF16)
    if forward:
        qc = jnp.concatenate([zeros, (q[half:] * jnp.exp2(acc[half:] - a_cross)).astype(BF16)], axis=0)
        kc = jnp.concatenate([(kk[:half] * jnp.exp2(a_cross - acc[:half])).astype(BF16), zeros], axis=0)
    else:
        qc = jnp.concatenate([(q[:half] * jnp.exp2(acc[:half] - a_cross)).astype(BF16), zeros], axis=0)
        kc = jnp.concatenate([zeros, (kk[half:] * jnp.exp2(a_cross - acc[half:])).astype(BF16)], axis=0)
    qi = (q * jnp.exp2(acc)).astype(BF16)
    ks = (kk * jnp.exp2(tot - acc)).astype(BF16)
    dec = jnp.exp2(tot)

    ti = lax.broadcasted_iota(jnp.int32, (2 * C, C), 0) % C
    si = lax.broadcasted_iota(jnp.int32, (2 * C, C), 1)
    causal = (ti >= si) if forward else (si >= ti)
    leaf_mask = jnp.logical_and(causal, (ti < half) == (si < half))
    first = lax.broadcasted_iota(jnp.int32, (C, LANES), 1) < C_HEAD_DIM
    head0 = jnp.where(first, 1.0, 0.0).astype(BF16)
    head1 = jnp.where(first, 0.0, 1.0).astype(BF16)
    r2 = lax.broadcasted_iota(jnp.int32, (LANES, LANES), 0) < C_HEAD_DIM
    c2 = lax.broadcasted_iota(jnp.int32, (LANES, LANES), 1) < C_HEAD_DIM
    same_head = r2 == c2

    def by_head(x):
        return jnp.concatenate([x * head0, x * head1], axis=0)

    vb = v.astype(BF16)
    for p in range(C_WIDTH // LANES):
        sl = slice(p * LANES, (p + 1) * LANES)
        s = (jnp.where(leaf_mask, _dot_nt(by_head(ql[:, sl]), kl[:, sl]), 0.0)
             + _dot_nt(by_head(qc[:, sl]), kc[:, sl])).astype(BF16)
        intra = _dot(jnp.concatenate([s[:C], s[C:]], axis=1), by_head(vb[:, sl]))
        st = st_ref[p]
        inter = _dot_nt(qi[:, sl], st.astype(BF16))
        o_ref[0, rows, sl] = (intra + inter).astype(o_ref.dtype)
        upd = _dot_tn(vb[:, sl], ks[:, sl])
        st_ref[p] = st * dec[:, sl] + jnp.where(same_head, upd, 0.0)


def _hgrn_kernel(qf_ref, zf_ref, vf_ref, qb_ref, zb_ref, vb_ref, lb_ref, tri_ref, of_ref, ob_ref, stf, stb, *, nsub):
    @pl.when(pl.program_id(1) == 0)
    def _():
        stf[...] = jnp.zeros_like(stf)
        stb[...] = jnp.zeros_like(stb)

    lbf, lbb = lb_ref[0:1, :], lb_ref[1:2, :]
    for j in range(nsub):
        rf = slice(j * GLA_CHUNK, (j + 1) * GLA_CHUNK)
        _gla_chunk(qf_ref[0, rf, :].astype(F32), zf_ref[0, rf, :].astype(F32), vf_ref[0, rf, :].astype(F32),
                   lbf, tri_ref[0], stf, of_ref, rf, True)
        rb = slice((nsub - 1 - j) * GLA_CHUNK, (nsub - j) * GLA_CHUNK)
        _gla_chunk(qb_ref[0, rb, :].astype(F32), zb_ref[0, rb, :].astype(F32), vb_ref[0, rb, :].astype(F32),
                   lbb, tri_ref[1], stb, ob_ref, rb, False)


def _hgrn(c_all, lb, rb):
    bsz, L, _ = c_all.shape
    nblk = L // rb
    w = C_WIDTH
    fwd = lambda col: pl.BlockSpec((1, rb, w), lambda b, i: (b, i, col))
    bwd = lambda col: pl.BlockSpec((1, rb, w), lambda b, i: (b, nblk - 1 - i, col))
    npair = w // LANES
    idx = np.arange(GLA_CHUNK)
    tri = jnp.asarray(np.stack([idx[:, None] >= idx[None, :], idx[None, :] >= idx[:, None]]), BF16)
    return pl.pallas_call(
        functools.partial(_hgrn_kernel, nsub=rb // GLA_CHUNK),
        grid=(bsz, nblk),
        in_specs=[fwd(0), fwd(1), fwd(3), bwd(0), bwd(2), bwd(3), pl.BlockSpec((2, w), lambda b, i: (0, 0)),
                  pl.BlockSpec((2, GLA_CHUNK, GLA_CHUNK), lambda b, i: (0, 0, 0))],
        out_specs=[pl.BlockSpec((1, rb, w), lambda b, i: (b, i, 0)),
                   pl.BlockSpec((1, rb, w), lambda b, i: (b, nblk - 1 - i, 0))],
        out_shape=[jax.ShapeDtypeStruct((bsz, L, w), BF16), jax.ShapeDtypeStruct((bsz, L, w), BF16)],
        scratch_shapes=[pltpu.VMEM((npair, LANES, LANES), F32), pltpu.VMEM((npair, LANES, LANES), F32)],
        compiler_params=_params("arbitrary", "arbitrary"),
        name="hgrn",
    )(c_all, c_all, c_all, c_all, c_all, c_all, lb, tri)


def _gelu_tanh(x):
    return 0.5 * x * (1.0 + jnp.tanh(math.sqrt(2.0 / math.pi) * (x + 0.044715 * (x * x * x))))


def _out_kernel(x_ref, mod_ref, ys_ref, au_ref, ag_ref, hy_ref, of_ref, ob_ref, cg_ref, d_ref, gw_ref, gb_ref,
                hn_ref, ones_ref, wo_ref, fn_ref, o_ref, *, final):
    y_rows, g_rows, b_rows = [], [], []
    for j in range(ys_ref.shape[0]):
        y_rows.append((ys_ref[j] + au_ref[j] * d_ref[...]).T)
        g_rows.append(ag_ref[j].astype(F32).T)
        b_rows.append(hy_ref[j].astype(F32).T)
    z = _gelu_tanh(jnp.concatenate(y_rows, axis=0))
    glu = _sigmoid(_dot(z.astype(BF16), gw_ref[...]) + gb_ref[...])
    a_out = (z * glu * _silu(jnp.concatenate(g_rows, axis=0))).astype(BF16)
    b_out = jnp.concatenate(b_rows, axis=0).astype(BF16)

    o = of_ref[0].astype(F32) + ob_ref[0].astype(F32)
    o2 = o * o
    hi, lo = _split_bf16(o2)
    ms = (_dot(hi, ones_ref[...]) + _dot(lo, ones_ref[...])) * (1.0 / C_HEAD_DIM)
    c_out = (o * lax.rsqrt(ms + EPS) * hn_ref[...] * _silu(cg_ref[0].astype(F32))).astype(BF16)

    mixed = _dot(jnp.concatenate([a_out, b_out, c_out], axis=1), wo_ref[...])
    xn = x_ref[0] + mod_ref[0, 2:3, :] * mixed
    if final:
        msq = jnp.mean(xn * xn, axis=-1, keepdims=True)
        xn = xn * lax.rsqrt(msq + EPS) * fn_ref[...]
    o_ref[0] = xn


def _out(x, mod, ys3, au3, pt3, hy3, of, ob, c_all, d_rep, gw, gb, hn, ones_bd, wo, fn, tb, final):
    bsz, L, dm = x.shape
    nt = tb // LANES
    row = lambda w, col: pl.BlockSpec((1, tb, w), lambda b, i: (b, i, col))
    ct = lambda ch, blk: pl.BlockSpec((nt, ch, LANES), lambda b, i: (i, blk, b))
    const = lambda b, i: (0, 0)
    full = lambda a: pl.BlockSpec(a.shape, const)
    return pl.pallas_call(
        functools.partial(_out_kernel, final=final),
        grid=(bsz, L // tb),
        in_specs=[row(dm, 0), pl.BlockSpec((1, 3, dm), lambda b, i: (b, 0, 0)),
                  ct(A_WIDTH, 0), ct(A_WIDTH, 0), ct(A_WIDTH, CT_AG // A_WIDTH), ct(B_WIDTH, 0),
                  row(C_WIDTH, 0), row(C_WIDTH, 0), row(C_WIDTH, 4),
                  full(d_rep), full(gw), full(gb), full(hn), full(ones_bd), full(wo), full(fn)],
        out_specs=row(dm, 0),
        out_shape=jax.ShapeDtypeStruct((bsz, L, dm), F32),
        compiler_params=_params("arbitrary", "arbitrary"),
        name="out",
    )(x, mod, ys3, au3, pt3, hy3, of, ob, c_all, d_rep, gw, gb, hn, ones_bd, wo, fn)


def _trunk(x, ada_rows, p, tb=512, hg_rb=512):
    bsz, L, _ = x.shape
    nchunk = L // LANES
    head = np.arange(C_WIDTH) // C_HEAD_DIM
    ones_bd = jnp.asarray(head[:, None] == head[None, :], BF16)
    for layer in range(DEPTH):
        mod = ada_rows[layer].reshape(bsz, 3, D_MODEL)
        w_in = p["w_in"][layer]
        o_b = 2 * A_WIDTH + 4 * B_WIDTH
        wt = w_in[:, :o_b].T.astype(BF16)
        wc = w_in[:, o_b:].astype(BF16)
        au, pt, c_all = _proj(x, mod, p["norm_w"][layer].reshape(1, -1).astype(F32), wt, wc)

        ys4 = _s5(au.reshape(nchunk, A_WIDTH, bsz, LANES), p["s5_consts"][layer])
        hy2 = _hyena(pt.reshape(nchunk, CT_ROWS * bsz * LANES), p["hy_filters"][(layer, L)], p["hy_conv_w"][layer],
                     p["hy_conv_b"][layer], p["hy_skip"][layer], L, bsz)
        of, ob = _hgrn(c_all, p["lower_bounds"][layer], hg_rb)

        rep = lambda v: jnp.repeat(v.astype(F32).reshape(-1, 1), LANES, axis=1)
        x = _out(x, mod, ys4.reshape(nchunk, A_WIDTH, bsz * LANES), au, pt, hy2.reshape(nchunk, B_WIDTH, bsz * LANES),
                 of, ob, c_all, rep(p["s5_d"][layer]), p["s5_glu_w"][layer].astype(BF16),
                 p["s5_glu_b"][layer].reshape(1, -1).astype(F32),
                 jnp.tile(p["hg_norm_w"][layer].astype(F32), C_WIDTH // C_HEAD_DIM).reshape(1, -1),
                 ones_bd, p["w_out"][layer].astype(BF16), p["final_norm_w"].reshape(1, -1).astype(F32),
                 tb, layer == DEPTH - 1)
    return x


def kernel(x_prompt, x_sample, c_prompt, c_sample, norm_w, ada_w, ada_b, w_in, w_out, s5_lambda_re, s5_lambda_im, s5_log_dt, s5_b_re, s5_b_im, s5_c_re, s5_c_im, s5_d, s5_glu_w, s5_glu_b, hy_conv_w, hy_conv_b, hy_w1, hy_b1, hy_w2, hy_b2, hy_w3, hy_sin_freq, hy_log_decay, hy_skip, hg_lb_logits, hg_norm_w, final_norm_w):
    bp = x_prompt.shape[0]
    ada = _ada(jnp.concatenate([c_prompt, c_sample], axis=0).astype(F32), ada_w.astype(F32), ada_b.astype(F32))
    lb_soft = jax.nn.softmax(hg_lb_logits.astype(F32), axis=0)
    lower_bounds = jnp.cumsum(lb_soft, axis=0) - lb_soft[0:1]
    lengths = sorted({x_prompt.shape[1], x_sample.shape[1]})
    p = dict(
        norm_w=norm_w, w_in=w_in, w_out=w_out, s5_d=s5_d, s5_glu_w=s5_glu_w, s5_glu_b=s5_glu_b,
        hy_conv_w=hy_conv_w, hy_conv_b=hy_conv_b, hy_skip=hy_skip, hg_norm_w=hg_norm_w, final_norm_w=final_norm_w,
        lower_bounds=lower_bounds,
        s5_consts=[_s5_consts(s5_lambda_re[l], s5_lambda_im[l], s5_log_dt[l], s5_b_re[l], s5_b_im[l],
                              s5_c_re[l], s5_c_im[l]) for l in range(DEPTH)],
        hy_filters={(l, L): _hyena_filters(L, hy_w1[l], hy_b1[l], hy_w2[l], hy_b2[l], hy_w3[l], hy_sin_freq[l],
                                           hy_log_decay[l])
                    for l in range(DEPTH) for L in lengths},
    )
    y_prompt = _trunk(x_prompt, ada[:, :bp], p)
    y_sample = _trunk(x_sample, ada[:, bp:], p)
    return (y_prompt, y_sample)
```

```python
import functools
import math

import numpy as np
import jax
import jax.numpy as jnp
from jax import lax
from jax.experimental import pallas as pl
from jax.experimental.pallas import tpu as pltpu

F32 = jnp.float32
BF16 = jnp.bfloat16
HIGHEST = lax.Precision.HIGHEST

D_MODEL = 1024
DEPTH = 2
A_WIDTH = 256
A_GROUP_CH = 16
A_GROUPS = 16
A_STATE = 64
B_WIDTH = 384
HY_ORDER = 2
HY_BANDS = 16
HY_POS_FEAT = 1 + 2 * HY_BANDS
HY_FILT_HID = 64
C_WIDTH = 384
C_HEAD_DIM = 64
EPS = 1e-6
F_FLOOR = 1e-30

LANES = 128
SUBLANES = 8
GLA_CHUNK = 128
FILT_SEG = 8
VMEM_LIMIT = 56 * 1024 * 1024

CT_AG, CT_BU, CT_BG = 0, A_WIDTH, A_WIDTH + 3 * B_WIDTH
CT_ROWS = A_WIDTH + 4 * B_WIDTH
NAT_COLS = 5 * C_WIDTH


def _dot(a, b, precision=None):
    return jnp.dot(a, b, preferred_element_type=F32, precision=precision)


def _dot_nt(a, b, precision=None):
    return lax.dot_general(a, b, (((1,), (1,)), ((), ())), preferred_element_type=F32, precision=precision)


def _dot_tn(a, b, precision=None):
    return lax.dot_general(a, b, (((0,), (0,)), ((), ())), preferred_element_type=F32, precision=precision)


def _split_bf16(a):
    hi = a.astype(BF16)
    return hi, (a - hi.astype(F32)).astype(BF16)


def _sigmoid(x):
    return 1.0 / (1.0 + jnp.exp(-x))


def _silu(x):
    return x * _sigmoid(x)


def _params(*sem):
    return pltpu.CompilerParams(dimension_semantics=sem, vmem_limit_bytes=VMEM_LIMIT)


def _resident(shape, index_map):
    return pl.BlockSpec(shape, index_map, pipeline_mode=pl.Buffered(1))


def _ada_kernel(c_ref, w_ref, b_ref, o_ref):
    cond = _silu(c_ref[...])
    o_ref[0] = _dot(cond, w_ref[0], precision=HIGHEST) + b_ref[0]


def _ada(c_all, ada_w, ada_b):
    nb = c_all.shape[0]
    d = D_MODEL
    return pl.pallas_call(
        _ada_kernel,
        grid=(DEPTH, 3),
        in_specs=[pl.BlockSpec((nb, d), lambda l, j: (0, 0)),
                  pl.BlockSpec((1, d, d), lambda l, j: (l, 0, j)),
                  pl.BlockSpec((1, 1, d), lambda l, j: (l, 0, j))],
        out_specs=pl.BlockSpec((1, nb, d), lambda l, j: (l, 0, j)),
        out_shape=jax.ShapeDtypeStruct((DEPTH, nb, 3 * d), F32),
        compiler_params=_params("arbitrary", "arbitrary"),
        name="ada",
    )(c_all, ada_w, ada_b.reshape(DEPTH, 1, 3 * d))


def _proj_kernel(x_ref, mod_ref, nw_ref, wt_ref, wc_ref, au_ref, pt_ref, c_ref):
    bsz = x_ref.shape[0]
    nb = 2
    rows = 256
    for s in range(bsz // nb):
        bs = slice(s * nb, (s + 1) * nb)
        x = x_ref[bs]
        ms = jnp.mean(x * x, axis=-1, keepdims=True)
        h = x * lax.rsqrt(ms + EPS) * nw_ref[...]
        h = h * (1.0 + mod_ref[bs, 1:2, :]) + mod_ref[bs, 0:1, :]
        hb = h.astype(BF16).reshape(nb * LANES, D_MODEL)
        ls = slice(s * nb * LANES, (s + 1) * nb * LANES)
        au = _dot_nt(wt_ref[0:A_WIDTH, :], hb)
        for k in range(nb):
            au_ref[0, :, s * nb + k, :] = au[:, k * LANES:(k + 1) * LANES]
        for r in range(0, CT_ROWS, rows):
            pt_ref[0, r:r + rows, ls] = _dot_nt(wt_ref[A_WIDTH + r:A_WIDTH + r + rows, :], hb).astype(BF16)
        for k in range(NAT_COLS // C_WIDTH):
            sl = slice(k * C_WIDTH, (k + 1) * C_WIDTH)
            c_ref[bs, :, sl] = _dot(hb, wc_ref[:, sl]).astype(BF16).reshape(nb, LANES, C_WIDTH)


def _proj(x, mod, norm_w, wt, wc):
    bsz, L, d = x.shape
    nchunk = L // LANES
    return pl.pallas_call(
        _proj_kernel,
        grid=(nchunk,),
        in_specs=[pl.BlockSpec((bsz, LANES, d), lambda i: (0, i, 0)),
                  _resident((bsz, 3, d), lambda i: (0, 0, 0)),
                  _resident((1, d), lambda i: (0, 0)),
                  _resident(wt.shape, lambda i: (0, 0)),
                  _resident(wc.shape, lambda i: (0, 0))],
        out_specs=[pl.BlockSpec((1, A_WIDTH, bsz, LANES), lambda i: (i, 0, 0, 0)),
                   pl.BlockSpec((1, CT_ROWS, bsz * LANES), lambda i: (i, 0, 0)),
                   pl.BlockSpec((bsz, LANES, NAT_COLS), lambda i: (0, i, 0))],
        out_shape=[jax.ShapeDtypeStruct((nchunk, A_WIDTH, bsz, LANES), F32),
                   jax.ShapeDtypeStruct((nchunk, CT_ROWS, bsz * LANES), BF16),
                   jax.ShapeDtypeStruct((bsz, L, NAT_COLS), BF16)],
        compiler_params=_params("arbitrary"),
        name="proj",
    )(x, mod, norm_w, wt, wc)


def _s5_consts(lam_re, lam_im, log_dt, b_re, b_im, c_re, c_im):
    T = LANES
    G, P, H = A_GROUPS, A_STATE, A_GROUP_CH
    dt = jnp.exp(log_dt.astype(F32))[..., None]
    lr, li = lam_re.astype(F32) * dt, lam_im.astype(F32) * dt

    def power(k):
        mag = jnp.exp(lr * k)
        return mag * jnp.cos(li * k), mag * jnp.sin(li * k)

    def cmul(ar, ai, br, bi):
        return ar * br - ai * bi, ar * bi + ai * br

    p1r, p1i = power(1.0)
    den = lam_re.astype(F32) ** 2 + lam_im.astype(F32) ** 2
    cfr, cfi = cmul(p1r - 1.0, p1i, lam_re.astype(F32) / den, -lam_im.astype(F32) / den)
    bbr, bbi = cmul(cfr[..., None], cfi[..., None], b_re.astype(F32)[None], b_im.astype(F32)[None])
    cr, ci = c_re.astype(F32), c_im.astype(F32)

    k = jnp.arange(T + 1, dtype=F32)[:, None, None, None]
    pwr, pwi = power(k)
    cpr, cpi = cmul(cr[None], ci[None], pwr[:T, :, :, None, :], pwi[:T, :, :, None, :])
    kern = (jnp.einsum("kdghp,dgpi->dkghi", cpr, bbr, precision=HIGHEST)
            - jnp.einsum("kdghp,dgpi->dkghi", cpi, bbi, precision=HIGHEST))
    two = jnp.concatenate([kern[1][:0:-1], (kern[0][0] + kern[1][0])[None], kern[0][1:],
                           jnp.zeros((1,) + kern.shape[2:], F32)], axis=0)
    k2 = two.transpose(1, 3, 2, 0).reshape(G, H * H, 2 * T)

    sf_r, sf_i = cmul(pwr[:T][::-1, 0, :, :, None], pwi[:T][::-1, 0, :, :, None], bbr[0][None], bbi[0][None])
    sb_r, sb_i = cmul(pwr[:T, 1, :, :, None], pwi[:T, 1, :, :, None], bbr[1][None], bbi[1][None])
    bmat = jnp.concatenate([sf_r, sb_r, sf_i, sb_i], axis=2)
    bmat = bmat.transpose(1, 3, 0, 2).reshape(G, H * T, 4 * P)

    mf_r, mf_i = cmul(cr[0][None], ci[0][None], pwr[1:, 0, :, None, :], pwi[1:, 0, :, None, :])
    mb_r, mb_i = cmul(cr[1][None], ci[1][None], pwr[1:][::-1, 1, :, None, :], pwi[1:][::-1, 1, :, None, :])
    zero = jnp.zeros_like(mf_r)
    cf = jnp.concatenate([mf_r, zero, -mf_i, zero], axis=3)
    cb = jnp.concatenate([zero, mb_r, zero, -mb_i], axis=3)
    cf = cf.transpose(1, 3, 2, 0).reshape(G, 4 * P, H * T)
    cb = cb.transpose(1, 3, 2, 0).reshape(G, 4 * P, H * T)

    lam_t = jnp.stack([jnp.concatenate([pwr[T, 0], pwr[T, 1]], axis=-1),
                       jnp.concatenate([pwi[T, 0], pwi[T, 1]], axis=-1)], axis=1)
    return k2, bmat.astype(BF16), cf.astype(BF16), cb.astype(BF16), lam_t


def _s5_kernel(u_ref, k2_ref, b_ref, cf_ref, cb_ref, lam_ref, d_ref, y_ref, tt, a_scr, sr_scr, si_scr, xf_scr,
               xb_scr):
    nchunk = u_ref.shape[0]
    m = nchunk * SUBLANES
    H, T = A_GROUP_CH, LANES

    def toeplitz(hp, carry):
        rows = k2_ref[0, pl.ds(pl.multiple_of(hp * H, H), H), :]
        for h in range(H):
            x = jnp.broadcast_to(rows[h:h + 1, :], (T, 2 * T))
            r = pltpu.roll(x, 1, axis=1, stride=1, stride_axis=0)
            tt[pl.ds(pl.multiple_of(hp * T, T), T), h * T:(h + 1) * T] = r[:, T:].astype(BF16)
        return carry

    lax.fori_loop(0, H, toeplitz, 0)

    for hp in range(H):
        a_scr[:, hp * T:(hp + 1) * T] = u_ref[:, hp].reshape(m, T).astype(BF16)
    s = _dot(a_scr[...], b_ref[0])
    sr_scr[...] = s[:, :LANES]
    si_scr[...] = s[:, LANES:]

    lam_r = jnp.broadcast_to(lam_ref[0, 0:1, :], (SUBLANES, LANES))
    lam_i = jnp.broadcast_to(lam_ref[0, 1:2, :], (SUBLANES, LANES))
    fwd_lane = lax.broadcasted_iota(jnp.int32, (SUBLANES, LANES), 1) < A_STATE

    def step(i, carry):
        xr, xi = carry
        rf = pl.ds(pl.multiple_of(i * SUBLANES, SUBLANES), SUBLANES)
        rb = pl.ds(pl.multiple_of((nchunk - 1 - i) * SUBLANES, SUBLANES), SUBLANES)
        xf_scr[rf, 0:LANES] = xr
        xf_scr[rf, LANES:2 * LANES] = xi
        xb_scr[rb, 0:LANES] = xr
        xb_scr[rb, LANES:2 * LANES] = xi
        sr = jnp.where(fwd_lane, sr_scr[rf, :], sr_scr[rb, :])
        si = jnp.where(fwd_lane, si_scr[rf, :], si_scr[rb, :])
        return lam_r * xr - lam_i * xi + sr, lam_r * xi + lam_i * xr + si

    zero = jnp.zeros((SUBLANES, LANES), F32)
    lax.fori_loop(0, nchunk, step, (zero, zero))

    a = a_scr[...]
    xf = xf_scr[...].astype(BF16)
    xb = xb_scr[...].astype(BF16)
    for j in range(H // 2):
        sl = slice(j * 2 * T, (j + 1) * 2 * T)
        yj = _dot(a, tt[:, sl]) + _dot(xf, cf_ref[0, :, sl]) + _dot(xb, cb_ref[0, :, sl])
        for k in range(2):
            h = 2 * j + k
            y_ref[:, h] = yj[:, k * T:(k + 1) * T].reshape(nchunk, SUBLANES, T) + u_ref[:, h] * d_ref[h]


def _s5(au4, consts, d4):
    k2, bmat, cf, cb, lam_t = consts
    nchunk, _, bsz, _ = au4.shape
    assert bsz == SUBLANES
    G, H, T = A_GROUPS, A_GROUP_CH, LANES
    m = nchunk * bsz
    per_g = lambda g: (g, 0, 0)
    return pl.pallas_call(
        _s5_kernel,
        grid=(G,),
        in_specs=[pl.BlockSpec((nchunk, H, bsz, T), lambda g: (0, g, 0, 0)),
                  pl.BlockSpec((1, H * H, 2 * T), per_g),
                  pl.BlockSpec((1, H * T, 4 * A_STATE), per_g),
                  pl.BlockSpec((1, 4 * A_STATE, H * T), per_g),
                  pl.BlockSpec((1, 4 * A_STATE, H * T), per_g),
                  pl.BlockSpec((1, 2, LANES), per_g),
                  pl.BlockSpec((H, bsz, T), per_g)],
        out_specs=pl.BlockSpec((nchunk, H, bsz, T), lambda g: (0, g, 0, 0)),
        out_shape=jax.ShapeDtypeStruct((nchunk, A_WIDTH, bsz, T), F32),
        scratch_shapes=[pltpu.VMEM((H * T, H * T), BF16), pltpu.VMEM((m, H * T), BF16),
                        pltpu.VMEM((m, LANES), F32), pltpu.VMEM((m, LANES), F32),
                        pltpu.VMEM((m, 2 * LANES), F32), pltpu.VMEM((m, 2 * LANES), F32)],
        compiler_params=_params("arbitrary"),
        name="s5",
    )(au4, k2, bmat, cf, cb, lam_t, d4)


def _dft_tables(L):
    N = 2 * L
    n2 = LANES
    n1 = N // n2
    h = n1 // 2
    k = np.arange(n1)
    f1 = np.exp(-2j * np.pi * np.outer(k, k) / n1)
    f2 = np.exp(-2j * np.pi * np.outer(np.arange(n2), np.arange(n2)) / n2)
    tw = np.exp(-2j * np.pi * np.outer(k, np.arange(n2)) / N)
    f1m = np.block([[f1.real[:, :h], -f1.imag[:, :h]], [f1.imag[:, :h], f1.real[:, :h]]])
    ffm = np.concatenate([f1.real, f1.imag], axis=0)
    f2m = np.block([[f2.real, f2.imag], [-f2.imag, f2.real]])
    g2m = np.block([[f2.real, -f2.imag], [f2.imag, f2.real]])
    g1 = np.conj(f1)[:h, :]
    g1m = np.block([[g1.real, -g1.imag], [g1.imag, g1.real]])
    return dict(n1=n1, f1m=f1m, ffm=ffm, f2m=f2m, g2m=g2m, g1m=g1m, twr=tw.real, twi=tw.imag)


def _filter_positions(L):
    N = 2 * L
    i = np.arange(N)
    pos = np.where(i < L, i, N - i).astype(np.float32)
    t = pos / np.float32(max(L - 1, 1))
    w = np.float32(2.0 * math.pi) * pos / np.float32(L)
    bands = np.linspace(1e-4, HY_BANDS - 1, HY_BANDS, dtype=np.float32)
    wb = (w[:, None] * bands).astype(np.float32)
    feats = np.concatenate([t[:, None], np.cos(wb), -np.sin(wb)], axis=-1).astype(np.float32)
    fpad = np.zeros((LANES, N), np.float32)
    fpad[:HY_POS_FEAT] = feats.T
    keep = np.ones((N,), np.float32)
    keep[L] = 0.0
    n1 = N // LANES
    return fpad, t.reshape(n1, 1, LANES), keep.reshape(n1, 1, LANES)


def _filt_kernel(feat_ref, w1t_ref, b1_ref, w2t_ref, b2_ref, w3t_ref, sf_ref, ld_ref, t_ref, keep_ref,
                 ts_ref, scale_ref, ss_scr, dec_scr, *, inv_n, half_steps):
    i = pl.program_id(0)

    @pl.when(i % half_steps == 0)
    def _():
        dec_scr[...] = jnp.exp(ld_ref[0])

    @pl.when(i == 0)
    def _():
        ss_scr[...] = jnp.zeros_like(ss_scr)

    h1 = jnp.sin(sf_ref[:, 0:1] * (_dot(w1t_ref[...], feat_ref[...], precision=HIGHEST) + b1_ref[...]))
    h2 = jnp.sin(sf_ref[:, 1:2] * (_dot(w2t_ref[...], h1, precision=HIGHEST) + b2_ref[...]))
    w_hi, w_lo = _split_bf16(w3t_ref[0])
    h_hi, h_lo = _split_bf16(h2)
    ss = ss_scr[...]
    for k in range(FILT_SEG):
        sl = slice(k * LANES, (k + 1) * LANES)
        h3t = _dot(w_hi, h_hi[:, sl]) + _dot(w_hi, h_lo[:, sl]) + _dot(w_lo, h_hi[:, sl])
        win = jnp.exp(-dec_scr[...] * t_ref[k])
        ts = h3t * win * keep_ref[k]
        ts_ref[k] = ts.astype(BF16)
        ss = ss + ts * ts
    ss_scr[...] = ss

    @pl.when(i == pl.num_programs(0) - 1)
    def _():
        tot = jnp.sum(ss_scr[...], axis=-1, keepdims=True)
        scale_ref[...] = lax.rsqrt(tot + EPS) * inv_n


def _filt_fft_kernel(ts_ref, ffm_ref, twr_ref, twi_ref, f2m_ref, hr_ref, hi_ref, *, n1, cb):
    tr, ti = twr_ref[...], twi_ref[...]
    group = 8
    for g in range(cb // group):
        a = _dot(ffm_ref[...], ts_ref[:, g * group * LANES:(g + 1) * group * LANES])
        for c in range(group):
            sl = slice(c * LANES, (c + 1) * LANES)
            ar, ai = a[:n1, sl], a[n1:, sl]
            cat = jnp.concatenate([ar * tr - ai * ti, ar * ti + ai * tr], axis=1).astype(BF16)
            x = _dot(cat, f2m_ref[...]).astype(BF16)
            out = slice((g * group + c) * LANES, (g * group + c + 1) * LANES)
            hr_ref[:, out] = x[:, :LANES]
            hi_ref[:, out] = x[:, LANES:]


def _hyena_filters(L, w1, b1, w2, b2, w3, sin_freq, log_decay, cb=32):
    tabs = _dft_tables(L)
    n1 = tabs["n1"]
    N = 2 * L
    nch = HY_ORDER * B_WIDTH
    fpad, tpos, keep = _filter_positions(L)
    w1t = jnp.zeros((HY_FILT_HID, LANES), F32).at[:, :HY_POS_FEAT].set(w1.astype(F32).T)
    w3t = w3.astype(F32).T.reshape(2, nch, HY_FILT_HID)
    ld = log_decay.astype(F32).reshape(2, nch, 1)
    half = n1 // 2 // FILT_SEG
    col = lambda v: v.astype(F32).reshape(-1, 1)
    fixed = lambda shape: pl.BlockSpec(shape, lambda i: (0,) * len(shape))
    ts, scale = pl.pallas_call(
        functools.partial(_filt_kernel, inv_n=1.0 / N, half_steps=half),
        grid=(n1 // FILT_SEG,),
        in_specs=[pl.BlockSpec((LANES, FILT_SEG * LANES), lambda i: (0, i)),
                  fixed((HY_FILT_HID, LANES)), fixed((HY_FILT_HID, 1)),
                  fixed((HY_FILT_HID, HY_FILT_HID)), fixed((HY_FILT_HID, 1)),
                  pl.BlockSpec((1, nch, HY_FILT_HID), lambda i: (i // half, 0, 0)),
                  fixed((HY_FILT_HID, 2)),
                  pl.BlockSpec((1, nch, 1), lambda i: (i // half, 0, 0)),
                  pl.BlockSpec((FILT_SEG, 1, LANES), lambda i: (i, 0, 0)),
                  pl.BlockSpec((FILT_SEG, 1, LANES), lambda i: (i, 0, 0))],
        out_specs=[pl.BlockSpec((FILT_SEG, nch, LANES), lambda i: (i, 0, 0)),
                   pl.BlockSpec((nch, 1), lambda i: (0, 0))],
        out_shape=[jax.ShapeDtypeStruct((n1, nch, LANES), BF16),
                   jax.ShapeDtypeStruct((nch, 1), F32)],
        scratch_shapes=[pltpu.VMEM((nch, LANES), F32), pltpu.VMEM((nch, 1), F32)],
        compiler_params=_params("arbitrary"),
        name="hy_filter",
    )(jnp.asarray(fpad), w1t, col(b1), w2.astype(F32).T, col(b2), w3t, sin_freq.astype(F32).T, ld,
      jnp.asarray(tpos), jnp.asarray(keep))
    W = cb * LANES
    hr, hi = pl.pallas_call(
        functools.partial(_filt_fft_kernel, n1=n1, cb=cb),
        grid=(nch // cb,),
        in_specs=[pl.BlockSpec((n1, W), lambda j: (0, j)),
                  pl.BlockSpec((2 * n1, n1), lambda j: (0, 0)),
                  pl.BlockSpec((n1, LANES), lambda j: (0, 0)),
                  pl.BlockSpec((n1, LANES), lambda j: (0, 0)),
                  pl.BlockSpec((2 * LANES, 2 * LANES), lambda j: (0, 0))],
        out_specs=[pl.BlockSpec((n1, W), lambda j: (0, j)),
                   pl.BlockSpec((n1, W), lambda j: (0, j))],
        out_shape=[jax.ShapeDtypeStruct((n1, nch * LANES), BF16),
                   jax.ShapeDtypeStruct((n1, nch * LANES), BF16)],
        compiler_params=_params("arbitrary"),
        name="hy_filter_fft",
    )(ts.reshape(n1, nch * LANES), jnp.asarray(tabs["ffm"], F32).astype(BF16), jnp.asarray(tabs["twr"], F32),
      jnp.asarray(tabs["twi"], F32), jnp.asarray(tabs["f2m"], F32).astype(BF16))
    scale_rep = jnp.repeat(scale.reshape(HY_ORDER, B_WIDTH), LANES, axis=1)
    return hr, hi, scale_rep


def _shift_time(x, direction):
    rows = x.shape[0]
    lane = lax.broadcasted_iota(jnp.int32, x.shape, 1)
    row = lax.broadcasted_iota(jnp.int32, x.shape, 0)
    top = row[:SUBLANES]
    if direction > 0:
        r = pltpu.roll(x, 1, axis=1)
        e = pltpu.roll(r, 1, axis=0)
        edge = jnp.concatenate([jnp.where(top == 0, 0.0, e[:SUBLANES]), e[SUBLANES:]], axis=0)
        return jnp.where(lane == 0, edge, r)
    r = pltpu.roll(x, LANES - 1, axis=1)
    e = pltpu.roll(r, rows - 1, axis=0)
    edge = jnp.concatenate([e[:rows - SUBLANES], jnp.where(top == SUBLANES - 1, 0.0, e[rows - SUBLANES:])], axis=0)
    return jnp.where(lane == LANES - 1, edge, r)


def _hyena_kernel(g0_ref, g1_ref, z_ref, bg_ref, cw0_ref, cw1_ref, cwz_ref, skip_ref, scale_ref,
                  h0r_ref, h0i_ref, h1r_ref, h1i_ref, f1m_ref, g1m_ref, f2m_ref, g2m_ref, twr_ref, twi_ref,
                  o_ref, z_scr, g0_scr, g1_scr, a_scr, b_scr, *, n1, cb, bsz):
    nh = n1 // 2
    npair = bsz // 2
    tile = lambda i: slice(i * LANES, (i + 1) * LANES)

    def short_conv(src, cw_ref, dst):
        for c in range(cb):
            w0, w1, w2, bias = (cw_ref[k:k + 1, tile(c)] for k in range(4))
            for b in range(bsz):
                x = src[:, tile(c * bsz + b)].astype(F32)
                y = bias + w0 * _shift_time(x, 1) + w1 * x + w2 * _shift_time(x, -1)
                half, p = divmod(b, npair)
                dst[half * nh:(half + 1) * nh, tile(c * npair + p)] = y

    short_conv(g0_ref, cw0_ref, g0_scr)
    short_conv(g1_ref, cw1_ref, g1_scr)
    short_conv(z_ref, cwz_ref, z_scr)
    tr, ti = twr_ref[...].astype(BF16), twi_ref[...].astype(BF16)

    def long_conv(hr_ref, hi_ref):
        a_scr[...] = _dot(f1m_ref[...], z_scr[...].astype(BF16)).astype(BF16)
        for c in range(cb):
            hr, hi = hr_ref[:, tile(c)], hi_ref[:, tile(c)]
            for p in range(npair):
                sl = tile(c * npair + p)
                ar, ai = a_scr[0:n1, sl], a_scr[n1:2 * n1, sl]
                x = _dot(jnp.concatenate([ar * tr - ai * ti, ar * ti + ai * tr], axis=1), f2m_ref[...]).astype(BF16)
                xr, xi = x[:, :LANES], x[:, LANES:]
                y = _dot(jnp.concatenate([xr * hr - xi * hi, xr * hi + xi * hr], axis=1), g2m_ref[...]).astype(BF16)
                br, bi = y[:, :LANES], y[:, LANES:]
                b_scr[0:n1, sl] = br * tr + bi * ti
                b_scr[n1:2 * n1, sl] = bi * tr - br * ti
        return _dot(g1m_ref[...], b_scr[...])

    def per_channel(ref, row):
        return jnp.concatenate([ref[row:row + 1, tile(c)] for c in range(cb) for _ in range(npair)], axis=1)

    z0 = z_scr[...]
    z1 = g0_scr[...] * (long_conv(h0r_ref, h0i_ref) * per_channel(scale_ref, 0) + z0 * per_channel(skip_ref, 0))
    z_scr[...] = z1
    z2 = g1_scr[...] * (long_conv(h1r_ref, h1i_ref) * per_channel(scale_ref, 1) + z1 * per_channel(skip_ref, 1))
    z_scr[...] = z2
    for c in range(cb):
        for b in range(bsz):
            half, p = divmod(b, npair)
            gate = _silu(bg_ref[:, tile(c * bsz + b)].astype(F32))
            o_ref[:, tile(c * bsz + b)] = (z_scr[half * nh:(half + 1) * nh, tile(c * npair + p)] * gate).astype(BF16)


def _hyena(pt2, filt, conv_w, conv_b, skip, L, bsz, cb=4):
    hr, hi, scale_rep = filt
    tabs = _dft_tables(L)
    n1 = tabs["n1"]
    nh = n1 // 2
    assert nh % SUBLANES == 0, "sequence length must be a multiple of 1024"
    wd = cb * bsz * LANES
    wc = cb * LANES
    ws = cb * (bsz // 2) * LANES
    ncb = B_WIDTH // cb
    cw = jnp.repeat(jnp.concatenate([conv_w.astype(F32), conv_b.astype(F32)[None]], axis=0), LANES, axis=1)
    sk = jnp.repeat(skip.astype(F32), LANES, axis=1)
    data = lambda row0: pl.BlockSpec((nh, wd), lambda j: (0, row0 // cb + j))
    cws = lambda grp: pl.BlockSpec((4, wc), lambda j: (0, grp * ncb + j))
    hs = lambda order: pl.BlockSpec((n1, wc), lambda j: (0, order * ncb + j))
    const = lambda j: (0, 0)
    return pl.pallas_call(
        functools.partial(_hyena_kernel, n1=n1, cb=cb, bsz=bsz),
        grid=(ncb,),
        in_specs=[data(CT_BU), data(CT_BU + B_WIDTH), data(CT_BU + 2 * B_WIDTH), data(CT_BG),
                  cws(0), cws(1), cws(2),
                  pl.BlockSpec((2, wc), lambda j: (0, j)), pl.BlockSpec((2, wc), lambda j: (0, j)),
                  hs(0), hs(0), hs(1), hs(1),
                  pl.BlockSpec((2 * n1, n1), const), pl.BlockSpec((n1, 2 * n1), const),
                  pl.BlockSpec((2 * LANES, 2 * LANES), const), pl.BlockSpec((2 * LANES, 2 * LANES), const),
                  pl.BlockSpec((n1, LANES), const), pl.BlockSpec((n1, LANES), const)],
        out_specs=pl.BlockSpec((nh, wd), lambda j: (0, j)),
        out_shape=jax.ShapeDtypeStruct((nh, B_WIDTH * bsz * LANES), BF16),
        scratch_shapes=[pltpu.VMEM((n1, ws), F32), pltpu.VMEM((n1, ws), F32), pltpu.VMEM((n1, ws), F32),
                        pltpu.VMEM((2 * n1, ws), BF16), pltpu.VMEM((2 * n1, ws), BF16)],
        compiler_params=_params("arbitrary"),
        name="hyena",
    )(pt2, pt2, pt2, pt2, cw, cw, cw, sk, scale_rep, hr, hi, hr, hi,
      jnp.asarray(tabs["f1m"], F32).astype(BF16), jnp.asarray(tabs["g1m"], F32).astype(BF16),
      jnp.asarray(tabs["f2m"], F32).astype(BF16), jnp.asarray(tabs["g2m"], F32).astype(BF16),
      jnp.asarray(tabs["twr"], F32), jnp.asarray(tabs["twi"], F32))


def _gla_prep(q, z, v, lb, tri, forward):
    C = GLA_CHUNK
    half, quarter = C // 2, C // 4
    one_m = 1.0 - lb
    sig = _sigmoid(z)
    f = lb + one_m * sig
    logf = jnp.log2(jnp.maximum(f, F_FLOOR))
    kk = one_m * (1.0 - sig)
    l_hi, l_lo = _split_bf16(logf)
    acc = _dot(tri, l_hi) + _dot(tri, l_lo)
    if forward:
        tot = acc[C - 1:C]
        a_low, a_high, a_cross = acc[quarter - 1:quarter], acc[half + quarter - 1:half + quarter], acc[half - 1:half]
    else:
        tot = acc[0:1]
        a_low, a_high, a_cross = acc[quarter:quarter + 1], acc[half + quarter:half + quarter + 1], acc[half:half + 1]
    e_leaf = jnp.concatenate([acc[:half] - a_low, acc[half:] - a_high], axis=0)
    ql = (q * jnp.exp2(e_leaf)).astype(BF16)
    kl = (kk * jnp.exp2(-e_leaf)).astype(BF16)
    zeros = jnp.zeros((half, q.shape[1]), BF16)
    if forward:
        qc = jnp.concatenate([zeros, (q[half:] * jnp.exp2(acc[half:] - a_cross)).astype(BF16)], axis=0)
        kc = jnp.concatenate([(kk[:half] * jnp.exp2(a_cross - acc[:half])).astype(BF16), zeros], axis=0)
    else:
        qc = jnp.concatenate([(q[:half] * jnp.exp2(acc[:half] - a_cross)).astype(BF16), zeros], axis=0)
        kc = jnp.concatenate([zeros, (kk[half:] * jnp.exp2(a_cross - acc[half:])).astype(BF16)], axis=0)
    qi = (q * jnp.exp2(acc)).astype(BF16)
    ks = (kk * jnp.exp2(tot - acc)).astype(BF16)
    return ql, kl, qc, kc, qi, ks, jnp.exp2(tot), v.astype(BF16)


def _gla_matmuls(ops, st_ref, o_ref, rows, forward, p):
    ql, kl, qc, kc, qi, ks, dec, vb = ops
    C = GLA_CHUNK
    half = C // 2
    ti = lax.broadcasted_iota(jnp.int32, (2 * C, C), 0) % C
    si = lax.broadcasted_iota(jnp.int32, (2 * C, C), 1)
    causal = (ti >= si) if forward else (si >= ti)
    leaf_mask = jnp.logical_and(causal, (ti < half) == (si < half))
    first = lax.broadcasted_iota(jnp.int32, (C, LANES), 1) < C_HEAD_DIM
    head0 = jnp.where(first, 1.0, 0.0).astype(BF16)
    head1 = jnp.where(first, 0.0, 1.0).astype(BF16)
    r2 = lax.broadcasted_iota(jnp.int32, (LANES, LANES), 0) < C_HEAD_DIM
    c2 = lax.broadcasted_iota(jnp.int32, (LANES, LANES), 1) < C_HEAD_DIM
    same_head = r2 == c2

    def by_head(x):
        return jnp.concatenate([x * head0, x * head1], axis=0)

    s = (jnp.where(leaf_mask, _dot_nt(by_head(ql), kl), 0.0) + _dot_nt(by_head(qc), kc)).astype(BF16)
    intra = _dot(jnp.concatenate([s[:C], s[C:]], axis=1), by_head(vb))
    st = st_ref[p]
    inter = _dot_nt(qi, st.astype(BF16))
    o_ref[0, rows, p * LANES:(p + 1) * LANES] = (intra + inter).astype(o_ref.dtype)
    upd = _dot_tn(vb, ks)
    st_ref[p] = st * dec + jnp.where(same_head, upd, 0.0)


def _hgrn_kernel(qf_ref, zf_ref, vf_ref, qb_ref, zb_ref, vb_ref, lb_ref, tri_ref, of_ref, ob_ref, stf, stb, *, nsub):
    @pl.when(pl.program_id(1) == 0)
    def _():
        stf[...] = jnp.zeros_like(stf)
        stb[...] = jnp.zeros_like(stb)

    items = []
    for j in range(nsub):
        for p in range(C_WIDTH // LANES):
            items.append((qf_ref, zf_ref, vf_ref, of_ref, stf, 0, slice(j * GLA_CHUNK, (j + 1) * GLA_CHUNK), True, p))
            items.append((qb_ref, zb_ref, vb_ref, ob_ref, stb, 1,
                          slice((nsub - 1 - j) * GLA_CHUNK, (nsub - j) * GLA_CHUNK), False, p))

    def prep(item):
        q_ref, z_ref, v_ref, _, _, d, rows, forward, p = item
        sl = slice(p * LANES, (p + 1) * LANES)
        return _gla_prep(q_ref[0, rows, sl].astype(F32), z_ref[0, rows, sl].astype(F32),
                         v_ref[0, rows, sl].astype(F32), lb_ref[d:d + 1, sl], tri_ref[d], forward)

    ops = prep(items[0])
    for i, item in enumerate(items):
        nxt = prep(items[i + 1]) if i + 1 < len(items) else None
        _gla_matmuls(ops, item[4], item[3], item[6], item[7], item[8])
        ops = nxt


def _hgrn(c_all, lb, rb):
    bsz, L, _ = c_all.shape
    nblk = L // rb
    w = C_WIDTH
    fwd = lambda col: pl.BlockSpec((1, rb, w), lambda b, i: (b, i, col))
    bwd = lambda col: pl.BlockSpec((1, rb, w), lambda b, i: (b, nblk - 1 - i, col))
    npair = w // LANES
    idx = np.arange(GLA_CHUNK)
    tri = jnp.asarray(np.stack([idx[:, None] >= idx[None, :], idx[None, :] >= idx[:, None]]), BF16)
    return pl.pallas_call(
        functools.partial(_hgrn_kernel, nsub=rb // GLA_CHUNK),
        grid=(bsz, nblk),
        in_specs=[fwd(0), fwd(1), fwd(3), bwd(0), bwd(2), bwd(3), pl.BlockSpec((2, w), lambda b, i: (0, 0)),
                  pl.BlockSpec((2, GLA_CHUNK, GLA_CHUNK), lambda b, i: (0, 0, 0))],
        out_specs=[pl.BlockSpec((1, rb, w), lambda b, i: (b, i, 0)),
                   pl.BlockSpec((1, rb, w), lambda b, i: (b, nblk - 1 - i, 0))],
        out_shape=[jax.ShapeDtypeStruct((bsz, L, w), BF16), jax.ShapeDtypeStruct((bsz, L, w), BF16)],
        scratch_shapes=[pltpu.VMEM((npair, LANES, LANES), F32), pltpu.VMEM((npair, LANES, LANES), F32)],
        compiler_params=_params("arbitrary", "arbitrary"),
        name="hgrn",
    )(c_all, c_all, c_all, c_all, c_all, c_all, lb, tri)


def _gelu_tanh(x):
    return 0.5 * x * (1.0 + jnp.tanh(math.sqrt(2.0 / math.pi) * (x + 0.044715 * (x * x * x))))


def _out_kernel(x_ref, mod_ref, ys_ref, ag_ref, hy_ref, of_ref, ob_ref, cg_ref, gw_ref, gb_ref,
                hn_ref, ones_ref, wo_ref, fn_ref, o_ref, *, final):
    bsz = x_ref.shape[0]
    m = bsz * LANES
    y_rows, g_rows, b_rows = [], [], []
    for b in range(bsz):
        lanes = slice(b * LANES, (b + 1) * LANES)
        y_rows.append(ys_ref[0, :, b, :].T)
        g_rows.append(ag_ref[0, :, lanes].astype(F32).T)
        b_rows.append(hy_ref[0, :, lanes].astype(F32).T)
    z = _gelu_tanh(jnp.concatenate(y_rows, axis=0))
    glu = _sigmoid(_dot(z.astype(BF16), gw_ref[...]) + gb_ref[...])
    a_out = (z * glu * _silu(jnp.concatenate(g_rows, axis=0))).astype(BF16)
    b_out = jnp.concatenate(b_rows, axis=0).astype(BF16)

    o = (of_ref[...].astype(F32) + ob_ref[...].astype(F32)).reshape(m, C_WIDTH)
    o2 = o * o
    hi, lo = _split_bf16(o2)
    ms = (_dot(hi, ones_ref[...]) + _dot(lo, ones_ref[...])) * (1.0 / C_HEAD_DIM)
    gate_c = _silu(cg_ref[...].astype(F32)).reshape(m, C_WIDTH)
    c_out = (o * lax.rsqrt(ms + EPS) * hn_ref[...] * gate_c).astype(BF16)

    mixed = _dot(jnp.concatenate([a_out, b_out, c_out], axis=1), wo_ref[...])
    xn = x_ref[...] + mod_ref[:, 2:3, :] * mixed.reshape(bsz, LANES, D_MODEL)
    if final:
        msq = jnp.mean(xn * xn, axis=-1, keepdims=True)
        xn = xn * lax.rsqrt(msq + EPS) * fn_ref[...]
    o_ref[...] = xn


def _out(x, mod, ys4, pt3, hy3, of, ob, c_all, gw, gb, hn, ones_bd, wo, fn, final):
    bsz, L, dm = x.shape
    row = lambda w, col: pl.BlockSpec((bsz, LANES, w), lambda i: (0, i, col))
    full = lambda a: _resident(a.shape, lambda i: (0,) * a.ndim)
    return pl.pallas_call(
        functools.partial(_out_kernel, final=final),
        grid=(L // LANES,),
        in_specs=[row(dm, 0), full(mod),
                  pl.BlockSpec((1, A_WIDTH, bsz, LANES), lambda i: (i, 0, 0, 0)),
                  pl.BlockSpec((1, A_WIDTH, bsz * LANES), lambda i: (i, CT_AG // A_WIDTH, 0)),
                  pl.BlockSpec((1, B_WIDTH, bsz * LANES), lambda i: (i, 0, 0)),
                  row(C_WIDTH, 0), row(C_WIDTH, 0), row(C_WIDTH, 4),
                  full(gw), full(gb), full(hn), full(ones_bd), full(wo), full(fn)],
        out_specs=row(dm, 0),
        out_shape=jax.ShapeDtypeStruct((bsz, L, dm), F32),
        compiler_params=_params("arbitrary"),
        name="out",
    )(x, mod, ys4, pt3, hy3, of, ob, c_all, gw, gb, hn, ones_bd, wo, fn)


def _trunk(x, ada_rows, p, hg_rb=512):
    bsz, L, _ = x.shape
    nchunk = L // LANES
    head = np.arange(C_WIDTH) // C_HEAD_DIM
    ones_bd = jnp.asarray(head[:, None] == head[None, :], BF16)
    for layer in range(DEPTH):
        mod = ada_rows[layer].reshape(bsz, 3, D_MODEL)
        w_in = p["w_in"][layer]
        o_b = 2 * A_WIDTH + 4 * B_WIDTH
        wt = w_in[:, :o_b].T.astype(BF16)
        wc = w_in[:, o_b:].astype(BF16)
        au, pt, c_all = _proj(x, mod, p["norm_w"][layer].reshape(1, -1).astype(F32), wt, wc)

        d4 = jnp.broadcast_to(p["s5_d"][layer].astype(F32)[:, None, None], (A_WIDTH, bsz, LANES))
        ys4 = _s5(au, p["s5_consts"][layer], d4)
        hy2 = _hyena(pt.reshape(nchunk, CT_ROWS * bsz * LANES), p["hy_filters"][(layer, L)], p["hy_conv_w"][layer],
                     p["hy_conv_b"][layer], p["hy_skip"][layer], L, bsz)
        of, ob = _hgrn(c_all, p["lower_bounds"][layer], hg_rb)

        x = _out(x, mod, ys4, pt, hy2.reshape(nchunk, B_WIDTH, bsz * LANES), of, ob, c_all,
                 p["s5_glu_w"][layer].astype(BF16), p["s5_glu_b"][layer].reshape(1, -1).astype(F32),
                 jnp.tile(p["hg_norm_w"][layer].astype(F32), C_WIDTH // C_HEAD_DIM).reshape(1, -1),
                 ones_bd, p["w_out"][layer].astype(BF16), p["final_norm_w"].reshape(1, -1).astype(F32),
                 layer == DEPTH - 1)
    return x


def kernel(x_prompt, x_sample, c_prompt, c_sample, norm_w, ada_w, ada_b, w_in, w_out, s5_lambda_re, s5_lambda_im, s5_log_dt, s5_b_re, s5_b_im, s5_c_re, s5_c_im, s5_d, s5_glu_w, s5_glu_b, hy_conv_w, hy_conv_b, hy_w1, hy_b1, hy_w2, hy_b2, hy_w3, hy_sin_freq, hy_log_decay, hy_skip, hg_lb_logits, hg_norm_w, final_norm_w):
    bp = x_prompt.shape[0]
    ada = _ada(jnp.concatenate([c_prompt, c_sample], axis=0).astype(F32), ada_w.astype(F32), ada_b.astype(F32))
    lb_soft = jax.nn.softmax(hg_lb_logits.astype(F32), axis=0)
    lower_bounds = jnp.cumsum(lb_soft, axis=0) - lb_soft[0:1]
    lengths = sorted({x_prompt.shape[1], x_sample.shape[1]})
    p = dict(
        norm_w=norm_w, w_in=w_in, w_out=w_out, s5_d=s5_d, s5_glu_w=s5_glu_w, s5_glu_b=s5_glu_b,
        hy_conv_w=hy_conv_w, hy_conv_b=hy_conv_b, hy_skip=hy_skip, hg_norm_w=hg_norm_w, final_norm_w=final_norm_w,
        lower_bounds=lower_bounds,
        s5_consts=[_s5_consts(s5_lambda_re[l], s5_lambda_im[l], s5_log_dt[l], s5_b_re[l], s5_b_im[l],
                              s5_c_re[l], s5_c_im[l]) for l in range(DEPTH)],
        hy_filters={(l, L): _hyena_filters(L, hy_w1[l], hy_b1[l], hy_w2[l], hy_b2[l], hy_w3[l], hy_sin_freq[l],
                                           hy_log_decay[l])
                    for l in range(DEPTH) for L in lengths},
    )
    y_prompt = _trunk(x_prompt, ada[:, :bp], p)
    y_sample = _trunk(x_sample, ada[:, bp:], p)
    return (y_prompt, y_sample)
```

```python
import functools
import math

import numpy as np
import jax
import jax.numpy as jnp
from jax import lax
from jax.experimental import pallas as pl
from jax.experimental.pallas import tpu as pltpu

F32 = jnp.float32
BF16 = jnp.bfloat16
HIGHEST = lax.Precision.HIGHEST

D_MODEL = 1024
DEPTH = 2
A_WIDTH = 256
A_GROUP_CH = 16
A_GROUPS = 16
A_STATE = 64
B_WIDTH = 384
HY_ORDER = 2
HY_BANDS = 16
HY_POS_FEAT = 1 + 2 * HY_BANDS
HY_FILT_HID = 64
C_WIDTH = 384
C_HEAD_DIM = 64
EPS = 1e-6
F_FLOOR = 1e-30

LANES = 128
SUBLANES = 8
GLA_CHUNK = 128
FILT_SEG = 8
VMEM_LIMIT = 56 * 1024 * 1024

CT_AG, CT_BU, CT_BG = 0, A_WIDTH, A_WIDTH + 3 * B_WIDTH
CT_ROWS = A_WIDTH + 4 * B_WIDTH
NAT_COLS = 5 * C_WIDTH
NAT_PIECE = 768


def _dot(a, b, precision=None):
    return jnp.dot(a, b, preferred_element_type=F32, precision=precision)


def _dot_nt(a, b, precision=None):
    return lax.dot_general(a, b, (((1,), (1,)), ((), ())), preferred_element_type=F32, precision=precision)


def _dot_tn(a, b, precision=None):
    return lax.dot_general(a, b, (((0,), (0,)), ((), ())), preferred_element_type=F32, precision=precision)


def _split_bf16(a):
    hi = a.astype(BF16)
    return hi, (a - hi.astype(F32)).astype(BF16)


def _sigmoid(x):
    return 1.0 / (1.0 + jnp.exp(-x))


def _silu(x):
    return x * _sigmoid(x)


def _params(*sem):
    return pltpu.CompilerParams(dimension_semantics=sem, vmem_limit_bytes=VMEM_LIMIT)


def _resident(shape, index_map):
    return pl.BlockSpec(shape, index_map, pipeline_mode=pl.Buffered(1))


def _ada_kernel(c_ref, w_ref, b_ref, o_ref):
    cond = _silu(c_ref[...])
    o_ref[0] = _dot(cond, w_ref[0], precision=HIGHEST) + b_ref[0]


def _ada(c_all, ada_w, ada_b):
    nb = c_all.shape[0]
    d = D_MODEL
    return pl.pallas_call(
        _ada_kernel,
        grid=(DEPTH, 3),
        in_specs=[pl.BlockSpec((nb, d), lambda l, j: (0, 0)),
                  pl.BlockSpec((1, d, d), lambda l, j: (l, 0, j)),
                  pl.BlockSpec((1, 1, d), lambda l, j: (l, 0, j))],
        out_specs=pl.BlockSpec((1, nb, d), lambda l, j: (l, 0, j)),
        out_shape=jax.ShapeDtypeStruct((DEPTH, nb, 3 * d), F32),
        compiler_params=_params("arbitrary", "arbitrary"),
        name="ada",
    )(c_all, ada_w, ada_b.reshape(DEPTH, 1, 3 * d))


def _proj_kernel(x_ref, mod_ref, nw_ref, wt_ref, wc_ref, au_ref, pt_ref, c_ref):
    bsz = x_ref.shape[0]
    nb = 2
    rows = 256
    for s in range(bsz // nb):
        bs = slice(s * nb, (s + 1) * nb)
        x = x_ref[bs]
        ms = jnp.mean(x * x, axis=-1, keepdims=True)
        h = x * lax.rsqrt(ms + EPS) * nw_ref[...]
        h = h * (1.0 + mod_ref[bs, 1:2, :]) + mod_ref[bs, 0:1, :]
        hb = h.astype(BF16).reshape(nb * LANES, D_MODEL)
        ls = slice(s * nb * LANES, (s + 1) * nb * LANES)
        au = _dot_nt(wt_ref[0:A_WIDTH, :], hb)
        for k in range(nb):
            au_ref[0, :, s * nb + k, :] = au[:, k * LANES:(k + 1) * LANES]
        for r in range(0, CT_ROWS, rows):
            pt_ref[0, r:r + rows, ls] = _dot_nt(wt_ref[A_WIDTH + r:A_WIDTH + r + rows, :], hb).astype(BF16)
        for c0 in range(0, NAT_COLS, NAT_PIECE):
            sl = slice(c0, min(c0 + NAT_PIECE, NAT_COLS))
            c_ref[bs, :, sl] = _dot(hb, wc_ref[:, sl]).astype(BF16).reshape(nb, LANES, sl.stop - sl.start)


def _proj(x, mod, norm_w, wt, wc):
    bsz, L, d = x.shape
    nchunk = L // LANES
    return pl.pallas_call(
        _proj_kernel,
        grid=(nchunk,),
        in_specs=[pl.BlockSpec((bsz, LANES, d), lambda i: (0, i, 0)),
                  _resident((bsz, 3, d), lambda i: (0, 0, 0)),
                  _resident((1, d), lambda i: (0, 0)),
                  _resident(wt.shape, lambda i: (0, 0)),
                  _resident(wc.shape, lambda i: (0, 0))],
        out_specs=[pl.BlockSpec((1, A_WIDTH, bsz, LANES), lambda i: (i, 0, 0, 0)),
                   pl.BlockSpec((1, CT_ROWS, bsz * LANES), lambda i: (i, 0, 0)),
                   pl.BlockSpec((bsz, LANES, NAT_COLS), lambda i: (0, i, 0))],
        out_shape=[jax.ShapeDtypeStruct((nchunk, A_WIDTH, bsz, LANES), F32),
                   jax.ShapeDtypeStruct((nchunk, CT_ROWS, bsz * LANES), BF16),
                   jax.ShapeDtypeStruct((bsz, L, NAT_COLS), BF16)],
        compiler_params=_params("arbitrary"),
        name="proj",
    )(x, mod, norm_w, wt, wc)


def _s5_consts(lam_re, lam_im, log_dt, b_re, b_im, c_re, c_im):
    T = LANES
    G, P, H = A_GROUPS, A_STATE, A_GROUP_CH
    dt = jnp.exp(log_dt.astype(F32))[..., None]
    lr, li = lam_re.astype(F32) * dt, lam_im.astype(F32) * dt

    def power(k):
        mag = jnp.exp(lr * k)
        return mag * jnp.cos(li * k), mag * jnp.sin(li * k)

    def cmul(ar, ai, br, bi):
        return ar * br - ai * bi, ar * bi + ai * br

    p1r, p1i = power(1.0)
    den = lam_re.astype(F32) ** 2 + lam_im.astype(F32) ** 2
    cfr, cfi = cmul(p1r - 1.0, p1i, lam_re.astype(F32) / den, -lam_im.astype(F32) / den)
    bbr, bbi = cmul(cfr[..., None], cfi[..., None], b_re.astype(F32)[None], b_im.astype(F32)[None])
    cr, ci = c_re.astype(F32), c_im.astype(F32)

    k = jnp.arange(T + 1, dtype=F32)[:, None, None, None]
    pwr, pwi = power(k)
    cpr, cpi = cmul(cr[None], ci[None], pwr[:T, :, :, None, :], pwi[:T, :, :, None, :])
    kern = (jnp.einsum("kdghp,dgpi->dkghi", cpr, bbr, precision=HIGHEST)
            - jnp.einsum("kdghp,dgpi->dkghi", cpi, bbi, precision=HIGHEST))
    two = jnp.concatenate([kern[1][:0:-1], (kern[0][0] + kern[1][0])[None], kern[0][1:],
                           jnp.zeros((1,) + kern.shape[2:], F32)], axis=0)
    k2 = two.transpose(1, 3, 2, 0).reshape(G, H * H, 2 * T)

    sf_r, sf_i = cmul(pwr[:T][::-1, 0, :, :, None], pwi[:T][::-1, 0, :, :, None], bbr[0][None], bbi[0][None])
    sb_r, sb_i = cmul(pwr[:T, 1, :, :, None], pwi[:T, 1, :, :, None], bbr[1][None], bbi[1][None])
    bmat = jnp.concatenate([sf_r, sb_r, sf_i, sb_i], axis=2)
    bmat = bmat.transpose(1, 3, 0, 2).reshape(G, H * T, 4 * P)

    mf_r, mf_i = cmul(cr[0][None], ci[0][None], pwr[1:, 0, :, None, :], pwi[1:, 0, :, None, :])
    mb_r, mb_i = cmul(cr[1][None], ci[1][None], pwr[1:][::-1, 1, :, None, :], pwi[1:][::-1, 1, :, None, :])
    zero = jnp.zeros_like(mf_r)
    cf = jnp.concatenate([mf_r, zero, -mf_i, zero], axis=3)
    cb = jnp.concatenate([zero, mb_r, zero, -mb_i], axis=3)
    cf = cf.transpose(1, 3, 2, 0).reshape(G, 4 * P, H * T)
    cb = cb.transpose(1, 3, 2, 0).reshape(G, 4 * P, H * T)

    lam_t = jnp.stack([jnp.concatenate([pwr[T, 0], pwr[T, 1]], axis=-1),
                       jnp.concatenate([pwi[T, 0], pwi[T, 1]], axis=-1)], axis=1)
    return k2, bmat.astype(BF16), cf.astype(BF16), cb.astype(BF16), lam_t


def _s5_kernel(*refs, ntrunk):
    u_refs = refs[:ntrunk]
    k2_ref, b_ref, cf_ref, cb_ref, lam_ref, d_ref = refs[ntrunk:ntrunk + 6]
    y_refs = refs[ntrunk + 6:2 * ntrunk + 6]
    tt, a_scr, sr_scr, si_scr, xf_scr, xb_scr = refs[2 * ntrunk + 6:]
    H, T = A_GROUP_CH, LANES

    def toeplitz(hp, carry):
        rows = k2_ref[0, pl.ds(pl.multiple_of(hp * H, H), H), :]
        for h in range(H):
            x = jnp.broadcast_to(rows[h:h + 1, :], (T, 2 * T))
            r = pltpu.roll(x, 1, axis=1, stride=1, stride_axis=0)
            tt[pl.ds(pl.multiple_of(hp * T, T), T), h * T:(h + 1) * T] = r[:, T:].astype(BF16)
        return carry

    lax.fori_loop(0, H, toeplitz, 0)

    lam_r = jnp.broadcast_to(lam_ref[0, 0:1, :], (SUBLANES, LANES))
    lam_i = jnp.broadcast_to(lam_ref[0, 1:2, :], (SUBLANES, LANES))
    fwd_lane = lax.broadcasted_iota(jnp.int32, (SUBLANES, LANES), 1) < A_STATE
    zero = jnp.zeros((SUBLANES, LANES), F32)

    for u_ref, y_ref in zip(u_refs, y_refs):
        nchunk = u_ref.shape[0]
        m = nchunk * SUBLANES
        for hp in range(H):
            a_scr[0:m, hp * T:(hp + 1) * T] = u_ref[:, hp].reshape(m, T).astype(BF16)
        s = _dot(a_scr[0:m, :], b_ref[0])
        sr_scr[0:m, :] = s[:, :LANES]
        si_scr[0:m, :] = s[:, LANES:]

        def step(i, carry, nchunk=nchunk):
            xr, xi = carry
            rf = pl.ds(pl.multiple_of(i * SUBLANES, SUBLANES), SUBLANES)
            rb = pl.ds(pl.multiple_of((nchunk - 1 - i) * SUBLANES, SUBLANES), SUBLANES)
            xf_scr[rf, 0:LANES] = xr
            xf_scr[rf, LANES:2 * LANES] = xi
            xb_scr[rb, 0:LANES] = xr
            xb_scr[rb, LANES:2 * LANES] = xi
            sr = jnp.where(fwd_lane, sr_scr[rf, :], sr_scr[rb, :])
            si = jnp.where(fwd_lane, si_scr[rf, :], si_scr[rb, :])
            return lam_r * xr - lam_i * xi + sr, lam_r * xi + lam_i * xr + si

        lax.fori_loop(0, nchunk, step, (zero, zero))

        a = a_scr[0:m, :]
        xf = xf_scr[0:m, :].astype(BF16)
        xb = xb_scr[0:m, :].astype(BF16)
        for j in range(H // 2):
            sl = slice(j * 2 * T, (j + 1) * 2 * T)
            yj = _dot(a, tt[:, sl]) + _dot(xf, cf_ref[0, :, sl]) + _dot(xb, cb_ref[0, :, sl])
            for k in range(2):
                h = 2 * j + k
                y_ref[:, h] = yj[:, k * T:(k + 1) * T].reshape(nchunk, SUBLANES, T) + u_ref[:, h] * d_ref[h]


def _s5(au4s, consts, d4):
    k2, bmat, cf, cb, lam_t = consts
    bsz = au4s[0].shape[2]
    assert bsz == SUBLANES
    G, H, T = A_GROUPS, A_GROUP_CH, LANES
    m = max(a.shape[0] for a in au4s) * bsz
    per_g = lambda g: (g, 0, 0)
    data = lambda a: pl.BlockSpec((a.shape[0], H, bsz, T), lambda g: (0, g, 0, 0))
    return pl.pallas_call(
        functools.partial(_s5_kernel, ntrunk=len(au4s)),
        grid=(G,),
        in_specs=[data(a) for a in au4s] + [
            pl.BlockSpec((1, H * H, 2 * T), per_g),
            pl.BlockSpec((1, H * T, 4 * A_STATE), per_g),
            pl.BlockSpec((1, 4 * A_STATE, H * T), per_g),
            pl.BlockSpec((1, 4 * A_STATE, H * T), per_g),
            pl.BlockSpec((1, 2, LANES), per_g),
            pl.BlockSpec((H, bsz, T), per_g)],
        out_specs=[data(a) for a in au4s],
        out_shape=[jax.ShapeDtypeStruct(a.shape, F32) for a in au4s],
        scratch_shapes=[pltpu.VMEM((H * T, H * T), BF16), pltpu.VMEM((m, H * T), BF16),
                        pltpu.VMEM((m, LANES), F32), pltpu.VMEM((m, LANES), F32),
                        pltpu.VMEM((m, 2 * LANES), F32), pltpu.VMEM((m, 2 * LANES), F32)],
        compiler_params=_params("arbitrary"),
        name="s5",
    )(*au4s, k2, bmat, cf, cb, lam_t, d4)


def _dft_tables(L):
    N = 2 * L
    n2 = LANES
    n1 = N // n2
    h = n1 // 2
    k = np.arange(n1)
    f1 = np.exp(-2j * np.pi * np.outer(k, k) / n1)
    f2 = np.exp(-2j * np.pi * np.outer(np.arange(n2), np.arange(n2)) / n2)
    tw = np.exp(-2j * np.pi * np.outer(k, np.arange(n2)) / N)
    f1m = np.block([[f1.real[:, :h], -f1.imag[:, :h]], [f1.imag[:, :h], f1.real[:, :h]]])
    ffm = np.concatenate([f1.real, f1.imag], axis=0)
    f2m = np.block([[f2.real, f2.imag], [-f2.imag, f2.real]])
    g2m = np.block([[f2.real, -f2.imag], [f2.imag, f2.real]])
    g1 = np.conj(f1)[:h, :]
    g1m = np.block([[g1.real, -g1.imag], [g1.imag, g1.real]])
    return dict(n1=n1, f1m=f1m, ffm=ffm, f2m=f2m, g2m=g2m, g1m=g1m, twr=tw.real, twi=tw.imag)


def _filter_positions(L):
    N = 2 * L
    i = np.arange(N)
    pos = np.where(i < L, i, N - i).astype(np.float32)
    t = pos / np.float32(max(L - 1, 1))
    w = np.float32(2.0 * math.pi) * pos / np.float32(L)
    bands = np.linspace(1e-4, HY_BANDS - 1, HY_BANDS, dtype=np.float32)
    wb = (w[:, None] * bands).astype(np.float32)
    feats = np.concatenate([t[:, None], np.cos(wb), -np.sin(wb)], axis=-1).astype(np.float32)
    fpad = np.zeros((LANES, N), np.float32)
    fpad[:HY_POS_FEAT] = feats.T
    keep = np.ones((N,), np.float32)
    keep[L] = 0.0
    n1 = N // LANES
    return fpad, t.reshape(n1, 1, LANES), keep.reshape(n1, 1, LANES)


def _filt_kernel(feat_ref, w1t_ref, b1_ref, w2t_ref, b2_ref, w3t_ref, sf_ref, ld_ref, t_ref, keep_ref,
                 ts_ref, scale_ref, ss_scr, dec_scr, *, inv_n, half_steps):
    i = pl.program_id(0)

    @pl.when(i % half_steps == 0)
    def _():
        dec_scr[...] = jnp.exp(ld_ref[0])

    @pl.when(i == 0)
    def _():
        ss_scr[...] = jnp.zeros_like(ss_scr)

    h1 = jnp.sin(sf_ref[:, 0:1] * (_dot(w1t_ref[...], feat_ref[...], precision=HIGHEST) + b1_ref[...]))
    h2 = jnp.sin(sf_ref[:, 1:2] * (_dot(w2t_ref[...], h1, precision=HIGHEST) + b2_ref[...]))
    w_hi, w_lo = _split_bf16(w3t_ref[0])
    h_hi, h_lo = _split_bf16(h2)
    ss = ss_scr[...]
    for k in range(FILT_SEG):
        sl = slice(k * LANES, (k + 1) * LANES)
        h3t = _dot(w_hi, h_hi[:, sl]) + _dot(w_hi, h_lo[:, sl]) + _dot(w_lo, h_hi[:, sl])
        win = jnp.exp(-dec_scr[...] * t_ref[k])
        ts = h3t * win * keep_ref[k]
        ts_ref[k] = ts.astype(BF16)
        ss = ss + ts * ts
    ss_scr[...] = ss

    @pl.when(i == pl.num_programs(0) - 1)
    def _():
        tot = jnp.sum(ss_scr[...], axis=-1, keepdims=True)
        scale_ref[...] = lax.rsqrt(tot + EPS) * inv_n


def _filt_fft_kernel(ts_ref, ffm_ref, twr_ref, twi_ref, f2m_ref, hr_ref, hi_ref, *, n1, cb):
    tr, ti = twr_ref[...], twi_ref[...]
    group = 8
    for g in range(cb // group):
        a = _dot(ffm_ref[...], ts_ref[:, g * group * LANES:(g + 1) * group * LANES])
        for c in range(group):
            sl = slice(c * LANES, (c + 1) * LANES)
            ar, ai = a[:n1, sl], a[n1:, sl]
            cat = jnp.concatenate([ar * tr - ai * ti, ar * ti + ai * tr], axis=1).astype(BF16)
            x = _dot(cat, f2m_ref[...]).astype(BF16)
            out = slice((g * group + c) * LANES, (g * group + c + 1) * LANES)
            hr_ref[:, out] = x[:, :LANES]
            hi_ref[:, out] = x[:, LANES:]


def _hyena_filters(L, w1, b1, w2, b2, w3, sin_freq, log_decay, cb=32):
    tabs = _dft_tables(L)
    n1 = tabs["n1"]
    N = 2 * L
    nch = HY_ORDER * B_WIDTH
    fpad, tpos, keep = _filter_positions(L)
    w1t = jnp.zeros((HY_FILT_HID, LANES), F32).at[:, :HY_POS_FEAT].set(w1.astype(F32).T)
    w3t = w3.astype(F32).T.reshape(2, nch, HY_FILT_HID)
    ld = log_decay.astype(F32).reshape(2, nch, 1)
    half = n1 // 2 // FILT_SEG
    col = lambda v: v.astype(F32).reshape(-1, 1)
    fixed = lambda shape: pl.BlockSpec(shape, lambda i: (0,) * len(shape))
    ts, scale = pl.pallas_call(
        functools.partial(_filt_kernel, inv_n=1.0 / N, half_steps=half),
        grid=(n1 // FILT_SEG,),
        in_specs=[pl.BlockSpec((LANES, FILT_SEG * LANES), lambda i: (0, i)),
                  fixed((HY_FILT_HID, LANES)), fixed((HY_FILT_HID, 1)),
                  fixed((HY_FILT_HID, HY_FILT_HID)), fixed((HY_FILT_HID, 1)),
                  pl.BlockSpec((1, nch, HY_FILT_HID), lambda i: (i // half, 0, 0)),
                  fixed((HY_FILT_HID, 2)),
                  pl.BlockSpec((1, nch, 1), lambda i: (i // half, 0, 0)),
                  pl.BlockSpec((FILT_SEG, 1, LANES), lambda i: (i, 0, 0)),
                  pl.BlockSpec((FILT_SEG, 1, LANES), lambda i: (i, 0, 0))],
        out_specs=[pl.BlockSpec((FILT_SEG, nch, LANES), lambda i: (i, 0, 0)),
                   pl.BlockSpec((nch, 1), lambda i: (0, 0))],
        out_shape=[jax.ShapeDtypeStruct((n1, nch, LANES), BF16),
                   jax.ShapeDtypeStruct((nch, 1), F32)],
        scratch_shapes=[pltpu.VMEM((nch, LANES), F32), pltpu.VMEM((nch, 1), F32)],
        compiler_params=_params("arbitrary"),
        name="hy_filter",
    )(jnp.asarray(fpad), w1t, col(b1), w2.astype(F32).T, col(b2), w3t, sin_freq.astype(F32).T, ld,
      jnp.asarray(tpos), jnp.asarray(keep))
    W = cb * LANES
    hr, hi = pl.pallas_call(
        functools.partial(_filt_fft_kernel, n1=n1, cb=cb),
        grid=(nch // cb,),
        in_specs=[pl.BlockSpec((n1, W), lambda j: (0, j)),
                  pl.BlockSpec((2 * n1, n1), lambda j: (0, 0)),
                  pl.BlockSpec((n1, LANES), lambda j: (0, 0)),
                  pl.BlockSpec((n1, LANES), lambda j: (0, 0)),
                  pl.BlockSpec((2 * LANES, 2 * LANES), lambda j: (0, 0))],
        out_specs=[pl.BlockSpec((n1, W), lambda j: (0, j)),
                   pl.BlockSpec((n1, W), lambda j: (0, j))],
        out_shape=[jax.ShapeDtypeStruct((n1, nch * LANES), BF16),
                   jax.ShapeDtypeStruct((n1, nch * LANES), BF16)],
        compiler_params=_params("arbitrary"),
        name="hy_filter_fft",
    )(ts.reshape(n1, nch * LANES), jnp.asarray(tabs["ffm"], F32).astype(BF16), jnp.asarray(tabs["twr"], F32),
      jnp.asarray(tabs["twi"], F32), jnp.asarray(tabs["f2m"], F32).astype(BF16))
    scale_rep = jnp.repeat(scale.reshape(HY_ORDER, B_WIDTH), LANES, axis=1)
    return hr, hi, scale_rep


def _shift_time(x, direction):
    rows = x.shape[0]
    lane = lax.broadcasted_iota(jnp.int32, x.shape, 1)
    row = lax.broadcasted_iota(jnp.int32, x.shape, 0)
    top = row[:SUBLANES]
    if direction > 0:
        r = pltpu.roll(x, 1, axis=1)
        e = pltpu.roll(r, 1, axis=0)
        edge = jnp.concatenate([jnp.where(top == 0, 0.0, e[:SUBLANES]), e[SUBLANES:]], axis=0)
        return jnp.where(lane == 0, edge, r)
    r = pltpu.roll(x, LANES - 1, axis=1)
    e = pltpu.roll(r, rows - 1, axis=0)
    edge = jnp.concatenate([e[:rows - SUBLANES], jnp.where(top == SUBLANES - 1, 0.0, e[rows - SUBLANES:])], axis=0)
    return jnp.where(lane == LANES - 1, edge, r)


def _hyena_kernel(g0_ref, g1_ref, z_ref, bg_ref, cw0_ref, cw1_ref, cwz_ref, skip_ref, scale_ref,
                  h0r_ref, h0i_ref, h1r_ref, h1i_ref, f1m_ref, g1m_ref, f2m_ref, g2m_ref, twr_ref, twi_ref,
                  o_ref, z_scr, g0_scr, g1_scr, a_scr, b_scr, *, n1, cb, bsz):
    nh = n1 // 2
    npair = bsz // 2
    tile = lambda i: slice(i * LANES, (i + 1) * LANES)

    def short_conv(src, cw_ref, dst):
        for c in range(cb):
            w0, w1, w2, bias = (cw_ref[k:k + 1, tile(c)] for k in range(4))
            for b in range(bsz):
                x = src[:, tile(c * bsz + b)].astype(F32)
                y = bias + w0 * _shift_time(x, 1) + w1 * x + w2 * _shift_time(x, -1)
                half, p = divmod(b, npair)
                dst[half * nh:(half + 1) * nh, tile(c * npair + p)] = y

    short_conv(g0_ref, cw0_ref, g0_scr)
    short_conv(g1_ref, cw1_ref, g1_scr)
    short_conv(z_ref, cwz_ref, z_scr)
    tr, ti = twr_ref[...].astype(BF16), twi_ref[...].astype(BF16)

    def long_conv(hr_ref, hi_ref):
        a_scr[...] = _dot(f1m_ref[...], z_scr[...].astype(BF16)).astype(BF16)
        for c in range(cb):
            hr, hi = hr_ref[:, tile(c)], hi_ref[:, tile(c)]
            for p in range(npair):
                sl = tile(c * npair + p)
                ar, ai = a_scr[0:n1, sl], a_scr[n1:2 * n1, sl]
                x = _dot(jnp.concatenate([ar * tr - ai * ti, ar * ti + ai * tr], axis=1), f2m_ref[...]).astype(BF16)
                xr, xi = x[:, :LANES], x[:, LANES:]
                y = _dot(jnp.concatenate([xr * hr - xi * hi, xr * hi + xi * hr], axis=1), g2m_ref[...]).astype(BF16)
                br, bi = y[:, :LANES], y[:, LANES:]
                b_scr[0:n1, sl] = br * tr + bi * ti
                b_scr[n1:2 * n1, sl] = bi * tr - br * ti
        return _dot(g1m_ref[...], b_scr[...])

    def per_channel(ref, row):
        return jnp.concatenate([ref[row:row + 1, tile(c)] for c in range(cb) for _ in range(npair)], axis=1)

    z0 = z_scr[...]
    z1 = g0_scr[...] * (long_conv(h0r_ref, h0i_ref) * per_channel(scale_ref, 0) + z0 * per_channel(skip_ref, 0))
    z_scr[...] = z1
    z2 = g1_scr[...] * (long_conv(h1r_ref, h1i_ref) * per_channel(scale_ref, 1) + z1 * per_channel(skip_ref, 1))
    z_scr[...] = z2
    for c in range(cb):
        for b in range(bsz):
            half, p = divmod(b, npair)
            gate = _silu(bg_ref[:, tile(c * bsz + b)].astype(F32))
            o_ref[:, tile(c * bsz + b)] = (z_scr[half * nh:(half + 1) * nh, tile(c * npair + p)] * gate).astype(BF16)


def _hyena(pt2, filt, conv_w, conv_b, skip, L, bsz, cb=4):
    hr, hi, scale_rep = filt
    tabs = _dft_tables(L)
    n1 = tabs["n1"]
    nh = n1 // 2
    assert nh % SUBLANES == 0, "sequence length must be a multiple of 1024"
    wd = cb * bsz * LANES
    wc = cb * LANES
    ws = cb * (bsz // 2) * LANES
    ncb = B_WIDTH // cb
    cw = jnp.repeat(jnp.concatenate([conv_w.astype(F32), conv_b.astype(F32)[None]], axis=0), LANES, axis=1)
    sk = jnp.repeat(skip.astype(F32), LANES, axis=1)
    data = lambda row0: pl.BlockSpec((nh, wd), lambda j: (0, row0 // cb + j))
    cws = lambda grp: pl.BlockSpec((4, wc), lambda j: (0, grp * ncb + j))
    hs = lambda order: pl.BlockSpec((n1, wc), lambda j: (0, order * ncb + j))
    const = lambda j: (0, 0)
    return pl.pallas_call(
        functools.partial(_hyena_kernel, n1=n1, cb=cb, bsz=bsz),
        grid=(ncb,),
        in_specs=[data(CT_BU), data(CT_BU + B_WIDTH), data(CT_BU + 2 * B_WIDTH), data(CT_BG),
                  cws(0), cws(1), cws(2),
                  pl.BlockSpec((2, wc), lambda j: (0, j)), pl.BlockSpec((2, wc), lambda j: (0, j)),
                  hs(0), hs(0), hs(1), hs(1),
                  pl.BlockSpec((2 * n1, n1), const), pl.BlockSpec((n1, 2 * n1), const),
                  pl.BlockSpec((2 * LANES, 2 * LANES), const), pl.BlockSpec((2 * LANES, 2 * LANES), const),
                  pl.BlockSpec((n1, LANES), const), pl.BlockSpec((n1, LANES), const)],
        out_specs=pl.BlockSpec((nh, wd), lambda j: (0, j)),
        out_shape=jax.ShapeDtypeStruct((nh, B_WIDTH * bsz * LANES), BF16),
        scratch_shapes=[pltpu.VMEM((n1, ws), F32), pltpu.VMEM((n1, ws), F32), pltpu.VMEM((n1, ws), F32),
                        pltpu.VMEM((2 * n1, ws), BF16), pltpu.VMEM((2 * n1, ws), BF16)],
        compiler_params=_params("arbitrary"),
        name="hyena",
    )(pt2, pt2, pt2, pt2, cw, cw, cw, sk, scale_rep, hr, hi, hr, hi,
      jnp.asarray(tabs["f1m"], F32).astype(BF16), jnp.asarray(tabs["g1m"], F32).astype(BF16),
      jnp.asarray(tabs["f2m"], F32).astype(BF16), jnp.asarray(tabs["g2m"], F32).astype(BF16),
      jnp.asarray(tabs["twr"], F32), jnp.asarray(tabs["twi"], F32))


def _gla_prep(q, z, v, lb, tri, forward):
    C = GLA_CHUNK
    half, quarter = C // 2, C // 4
    one_m = 1.0 - lb
    sig = _sigmoid(z)
    f = lb + one_m * sig
    logf = jnp.log2(jnp.maximum(f, F_FLOOR))
    kk = one_m * (1.0 - sig)
    l_hi, l_lo = _split_bf16(logf)
    acc = _dot(tri, l_hi) + _dot(tri, l_lo)
    if forward:
        tot = acc[C - 1:C]
        a_low, a_high, a_cross = acc[quarter - 1:quarter], acc[half + quarter - 1:half + quarter], acc[half - 1:half]
    else:
        tot = acc[0:1]
        a_low, a_high, a_cross = acc[quarter:quarter + 1], acc[half + quarter:half + quarter + 1], acc[half:half + 1]
    e_leaf = jnp.concatenate([acc[:half] - a_low, acc[half:] - a_high], axis=0)
    ql = (q * jnp.exp2(e_leaf)).astype(BF16)
    kl = (kk * jnp.exp2(-e_leaf)).astype(BF16)
    zeros = jnp.zeros((half, q.shape[1]), BF16)
    if forward:
        qc = jnp.concatenate([zeros, (q[half:] * jnp.exp2(acc[half:] - a_cross)).astype(BF16)], axis=0)
        kc = jnp.concatenate([(kk[:half] * jnp.exp2(a_cross - acc[:half])).astype(BF16), zeros], axis=0)
    else:
        qc = jnp.concatenate([(q[:half] * jnp.exp2(acc[:half] - a_cross)).astype(BF16), zeros], axis=0)
        kc = jnp.concatenate([zeros, (kk[half:] * jnp.exp2(a_cross - acc[half:])).astype(BF16)], axis=0)
    qi = (q * jnp.exp2(acc)).astype(BF16)
    ks = (kk * jnp.exp2(tot - acc)).astype(BF16)
    return ql, kl, qc, kc, qi, ks, jnp.exp2(tot), v.astype(BF16)


def _gla_matmuls(ops, st_ref, o_ref, rows, forward, p):
    ql, kl, qc, kc, qi, ks, dec, vb = ops
    C = GLA_CHUNK
    half = C // 2
    ti = lax.broadcasted_iota(jnp.int32, (2 * C, C), 0) % C
    si = lax.broadcasted_iota(jnp.int32, (2 * C, C), 1)
    causal = (ti >= si) if forward else (si >= ti)
    leaf_mask = jnp.logical_and(causal, (ti < half) == (si < half))
    first = lax.broadcasted_iota(jnp.int32, (C, LANES), 1) < C_HEAD_DIM
    head0 = jnp.where(first, 1.0, 0.0).astype(BF16)
    head1 = jnp.where(first, 0.0, 1.0).astype(BF16)
    r2 = lax.broadcasted_iota(jnp.int32, (LANES, LANES), 0) < C_HEAD_DIM
    c2 = lax.broadcasted_iota(jnp.int32, (LANES, LANES), 1) < C_HEAD_DIM
    same_head = r2 == c2

    def by_head(x):
        return jnp.concatenate([x * head0, x * head1], axis=0)

    s = (jnp.where(leaf_mask, _dot_nt(by_head(ql), kl), 0.0) + _dot_nt(by_head(qc), kc)).astype(BF16)
    intra = _dot(jnp.concatenate([s[:C], s[C:]], axis=1), by_head(vb))
    st = st_ref[p]
    inter = _dot_nt(qi, st.astype(BF16))
    o_ref[0, rows, p * LANES:(p + 1) * LANES] = (intra + inter).astype(o_ref.dtype)
    upd = _dot_tn(vb, ks)
    st_ref[p] = st * dec + jnp.where(same_head, upd, 0.0)


def _hgrn_kernel(qf_ref, zf_ref, vf_ref, qb_ref, zb_ref, vb_ref, lb_ref, tri_ref, of_ref, ob_ref, stf, stb, *, nsub):
    @pl.when(pl.program_id(1) == 0)
    def _():
        stf[...] = jnp.zeros_like(stf)
        stb[...] = jnp.zeros_like(stb)

    items = []
    for j in range(nsub):
        for p in range(C_WIDTH // LANES):
            items.append((qf_ref, zf_ref, vf_ref, of_ref, stf, 0, slice(j * GLA_CHUNK, (j + 1) * GLA_CHUNK), True, p))
            items.append((qb_ref, zb_ref, vb_ref, ob_ref, stb, 1,
                          slice((nsub - 1 - j) * GLA_CHUNK, (nsub - j) * GLA_CHUNK), False, p))

    def prep(item):
        q_ref, z_ref, v_ref, _, _, d, rows, forward, p = item
        sl = slice(p * LANES, (p + 1) * LANES)
        return _gla_prep(q_ref[0, rows, sl].astype(F32), z_ref[0, rows, sl].astype(F32),
                         v_ref[0, rows, sl].astype(F32), lb_ref[d:d + 1, sl], tri_ref[d], forward)

    ops = prep(items[0])
    for i, item in enumerate(items):
        nxt = prep(items[i + 1]) if i + 1 < len(items) else None
        _gla_matmuls(ops, item[4], item[3], item[6], item[7], item[8])
        ops = nxt


def _hgrn(c_all, lb, rb):
    bsz, L, _ = c_all.shape
    nblk = L // rb
    w = C_WIDTH
    fwd = lambda col: pl.BlockSpec((1, rb, w), lambda b, i: (b, i, col))
    bwd = lambda col: pl.BlockSpec((1, rb, w), lambda b, i: (b, nblk - 1 - i, col))
    npair = w // LANES
    idx = np.arange(GLA_CHUNK)
    tri = jnp.asarray(np.stack([idx[:, None] >= idx[None, :], idx[None, :] >= idx[:, None]]), BF16)
    return pl.pallas_call(
        functools.partial(_hgrn_kernel, nsub=rb // GLA_CHUNK),
        grid=(bsz, nblk),
        in_specs=[fwd(0), fwd(1), fwd(3), bwd(0), bwd(2), bwd(3), pl.BlockSpec((2, w), lambda b, i: (0, 0)),
                  pl.BlockSpec((2, GLA_CHUNK, GLA_CHUNK), lambda b, i: (0, 0, 0))],
        out_specs=[pl.BlockSpec((1, rb, w), lambda b, i: (b, i, 0)),
                   pl.BlockSpec((1, rb, w), lambda b, i: (b, nblk - 1 - i, 0))],
        out_shape=[jax.ShapeDtypeStruct((bsz, L, w), BF16), jax.ShapeDtypeStruct((bsz, L, w), BF16)],
        scratch_shapes=[pltpu.VMEM((npair, LANES, LANES), F32), pltpu.VMEM((npair, LANES, LANES), F32)],
        compiler_params=_params("arbitrary", "arbitrary"),
        name="hgrn",
    )(c_all, c_all, c_all, c_all, c_all, c_all, lb, tri)


def _gelu_tanh(x):
    return 0.5 * x * (1.0 + jnp.tanh(math.sqrt(2.0 / math.pi) * (x + 0.044715 * (x * x * x))))


def _out_kernel(x_ref, mod_ref, ys_ref, ag_ref, hy_ref, of_ref, ob_ref, cg_ref, gw_ref, gb_ref,
                hn_ref, ones_ref, wo_ref, fn_ref, o_ref, *, final):
    bsz = x_ref.shape[0]
    m = bsz * LANES
    y_rows, g_rows, b_rows = [], [], []
    for b in range(bsz):
        lanes = slice(b * LANES, (b + 1) * LANES)
        y_rows.append(ys_ref[0, :, b, :].T)
        g_rows.append(ag_ref[0, :, lanes].astype(F32).T)
        b_rows.append(hy_ref[0, :, lanes].astype(F32).T)
    z = _gelu_tanh(jnp.concatenate(y_rows, axis=0))
    glu = _sigmoid(_dot(z.astype(BF16), gw_ref[...]) + gb_ref[...])
    a_out = (z * glu * _silu(jnp.concatenate(g_rows, axis=0))).astype(BF16)
    b_out = jnp.concatenate(b_rows, axis=0).astype(BF16)

    o = (of_ref[...].astype(F32) + ob_ref[...].astype(F32)).reshape(m, C_WIDTH)
    ms = _dot((o * o).astype(BF16), ones_ref[...]) * (1.0 / C_HEAD_DIM)
    gate_c = _silu(cg_ref[...].astype(F32)).reshape(m, C_WIDTH)
    c_out = (o * lax.rsqrt(ms + EPS) * hn_ref[...] * gate_c).astype(BF16)

    mixed = _dot(jnp.concatenate([a_out, b_out, c_out], axis=1), wo_ref[...])
    xn = x_ref[...] + mod_ref[:, 2:3, :] * mixed.reshape(bsz, LANES, D_MODEL)
    if final:
        msq = jnp.mean(xn * xn, axis=-1, keepdims=True)
        xn = xn * lax.rsqrt(msq + EPS) * fn_ref[...]
    o_ref[...] = xn


def _out(x, mod, ys4, pt3, hy3, of, ob, c_all, gw, gb, hn, ones_bd, wo, fn, final):
    bsz, L, dm = x.shape
    row = lambda w, col: pl.BlockSpec((bsz, LANES, w), lambda i: (0, i, col))
    full = lambda a: _resident(a.shape, lambda i: (0,) * a.ndim)
    return pl.pallas_call(
        functools.partial(_out_kernel, final=final),
        grid=(L // LANES,),
        in_specs=[row(dm, 0), full(mod),
                  pl.BlockSpec((1, A_WIDTH, bsz, LANES), lambda i: (i, 0, 0, 0)),
                  pl.BlockSpec((1, A_WIDTH, bsz * LANES), lambda i: (i, CT_AG // A_WIDTH, 0)),
                  pl.BlockSpec((1, B_WIDTH, bsz * LANES), lambda i: (i, 0, 0)),
                  row(C_WIDTH, 0), row(C_WIDTH, 0), row(C_WIDTH, 4),
                  full(gw), full(gb), full(hn), full(ones_bd), full(wo), full(fn)],
        out_specs=row(dm, 0),
        out_shape=jax.ShapeDtypeStruct((bsz, L, dm), F32),
        compiler_params=_params("arbitrary"),
        name="out",
    )(x, mod, ys4, pt3, hy3, of, ob, c_all, gw, gb, hn, ones_bd, wo, fn)


def _layer(xs, mods, p, layer, hg_rb=512):
    head = np.arange(C_WIDTH) // C_HEAD_DIM
    ones_bd = jnp.asarray(head[:, None] == head[None, :], BF16)
    w_in = p["w_in"][layer]
    o_b = 2 * A_WIDTH + 4 * B_WIDTH
    wt = w_in[:, :o_b].T.astype(BF16)
    wc = w_in[:, o_b:].astype(BF16)
    norm_w = p["norm_w"][layer].reshape(1, -1).astype(F32)
    projs = [_proj(x, mod, norm_w, wt, wc) for x, mod in zip(xs, mods)]
    bsz = xs[0].shape[0]
    d4 = jnp.broadcast_to(p["s5_d"][layer].astype(F32)[:, None, None], (A_WIDTH, bsz, LANES))
    ys4s = _s5([au for au, _, _ in projs], p["s5_consts"][layer], d4)
    out = []
    for x, mod, (au, pt, c_all), ys4 in zip(xs, mods, projs, ys4s):
        L = x.shape[1]
        nchunk = L // LANES
        hy2 = _hyena(pt.reshape(nchunk, CT_ROWS * bsz * LANES), p["hy_filters"][(layer, L)], p["hy_conv_w"][layer],
                     p["hy_conv_b"][layer], p["hy_skip"][layer], L, bsz)
        of, ob = _hgrn(c_all, p["lower_bounds"][layer], hg_rb)
        out.append(_out(x, mod, ys4, pt, hy2.reshape(nchunk, B_WIDTH, bsz * LANES), of, ob, c_all,
                        p["s5_glu_w"][layer].astype(BF16), p["s5_glu_b"][layer].reshape(1, -1).astype(F32),
                        jnp.tile(p["hg_norm_w"][layer].astype(F32), C_WIDTH // C_HEAD_DIM).reshape(1, -1),
                        ones_bd, p["w_out"][layer].astype(BF16), p["final_norm_w"].reshape(1, -1).astype(F32),
                        layer == DEPTH - 1))
    return out


def kernel(x_prompt, x_sample, c_prompt, c_sample, norm_w, ada_w, ada_b, w_in, w_out, s5_lambda_re, s5_lambda_im, s5_log_dt, s5_b_re, s5_b_im, s5_c_re, s5_c_im, s5_d, s5_glu_w, s5_glu_b, hy_conv_w, hy_conv_b, hy_w1, hy_b1, hy_w2, hy_b2, hy_w3, hy_sin_freq, hy_log_decay, hy_skip, hg_lb_logits, hg_norm_w, final_norm_w):
    bp = x_prompt.shape[0]
    ada = _ada(jnp.concatenate([c_prompt, c_sample], axis=0).astype(F32), ada_w.astype(F32), ada_b.astype(F32))
    lb_soft = jax.nn.softmax(hg_lb_logits.astype(F32), axis=0)
    lower_bounds = jnp.cumsum(lb_soft, axis=0) - lb_soft[0:1]
    lengths = sorted({x_prompt.shape[1], x_sample.shape[1]})
    p = dict(
        norm_w=norm_w, w_in=w_in, w_out=w_out, s5_d=s5_d, s5_glu_w=s5_glu_w, s5_glu_b=s5_glu_b,
        hy_conv_w=hy_conv_w, hy_conv_b=hy_conv_b, hy_skip=hy_skip, hg_norm_w=hg_norm_w, final_norm_w=final_norm_w,
        lower_bounds=lower_bounds,
        s5_consts=[_s5_consts(s5_lambda_re[l], s5_lambda_im[l], s5_log_dt[l], s5_b_re[l], s5_b_im[l],
                              s5_c_re[l], s5_c_im[l]) for l in range(DEPTH)],
        hy_filters={(l, L): _hyena_filters(L, hy_w1[l], hy_b1[l], hy_w2[l], hy_b2[l], hy_w3[l], hy_sin_freq[l],
                                           hy_log_decay[l])
                    for l in range(DEPTH) for L in lengths},
    )
    xs = [x_prompt, x_sample]
    for layer in range(DEPTH):
        mods = [ada[layer, :bp].reshape(bp, 3, D_MODEL),
                ada[layer, bp:].reshape(x_sample.shape[0], 3, D_MODEL)]
        xs = _layer(xs, mods, p, layer)
    return (xs[0], xs[1])
```

```python
import functools
import math

import numpy as np
import jax
import jax.numpy as jnp
from jax import lax
from jax.experimental import pallas as pl
from jax.experimental.pallas import tpu as pltpu

F32 = jnp.float32
BF16 = jnp.bfloat16
HIGHEST = lax.Precision.HIGHEST

D_MODEL = 1024
DEPTH = 2
A_WIDTH = 256
A_GROUP_CH = 16
A_GROUPS = 16
A_STATE = 64
B_WIDTH = 384
HY_ORDER = 2
HY_BANDS = 16
HY_POS_FEAT = 1 + 2 * HY_BANDS
HY_FILT_HID = 64
C_WIDTH = 384
C_HEAD_DIM = 64
EPS = 1e-6
F_FLOOR = 1e-30

LANES = 128
SUBLANES = 8
GLA_CHUNK = 128
FILT_SEG = 8
VMEM_LIMIT = 56 * 1024 * 1024

CT_AG, CT_BU, CT_BG = 0, A_WIDTH, A_WIDTH + 3 * B_WIDTH
CT_ROWS = A_WIDTH + 4 * B_WIDTH
NAT_COLS = 5 * C_WIDTH
NAT_PIECE = 768


def _dot(a, b, precision=None):
    return jnp.dot(a, b, preferred_element_type=F32, precision=precision)


def _dot_nt(a, b, precision=None):
    return lax.dot_general(a, b, (((1,), (1,)), ((), ())), preferred_element_type=F32, precision=precision)


def _dot_tn(a, b, precision=None):
    return lax.dot_general(a, b, (((0,), (0,)), ((), ())), preferred_element_type=F32, precision=precision)


def _split_bf16(a):
    hi = a.astype(BF16)
    return hi, (a - hi.astype(F32)).astype(BF16)


def _sigmoid(x):
    return 1.0 / (1.0 + jnp.exp(-x))


def _silu(x):
    return x * _sigmoid(x)


def _params(*sem):
    return pltpu.CompilerParams(dimension_semantics=sem, vmem_limit_bytes=VMEM_LIMIT)


def _resident(shape, index_map):
    return pl.BlockSpec(shape, index_map, pipeline_mode=pl.Buffered(1))


def _ada_kernel(c_ref, w_ref, b_ref, o_ref):
    cond = _silu(c_ref[...])
    o_ref[0] = _dot(cond, w_ref[0], precision=HIGHEST) + b_ref[0]


def _ada(c_all, ada_w, ada_b):
    nb = c_all.shape[0]
    d = D_MODEL
    return pl.pallas_call(
        _ada_kernel,
        grid=(DEPTH, 3),
        in_specs=[pl.BlockSpec((nb, d), lambda l, j: (0, 0)),
                  pl.BlockSpec((1, d, d), lambda l, j: (l, 0, j)),
                  pl.BlockSpec((1, 1, d), lambda l, j: (l, 0, j))],
        out_specs=pl.BlockSpec((1, nb, d), lambda l, j: (l, 0, j)),
        out_shape=jax.ShapeDtypeStruct((DEPTH, nb, 3 * d), F32),
        compiler_params=_params("arbitrary", "arbitrary"),
        name="ada",
    )(c_all, ada_w, ada_b.reshape(DEPTH, 1, 3 * d))


def _proj_kernel(x_ref, mod_ref, nw_ref, wt_ref, wc_ref, au_ref, pt_ref, c_ref):
    bsz = x_ref.shape[0]
    nb = 2
    rows = 256
    for s in range(bsz // nb):
        bs = slice(s * nb, (s + 1) * nb)
        x = x_ref[bs]
        ms = jnp.mean(x * x, axis=-1, keepdims=True)
        h = x * lax.rsqrt(ms + EPS) * nw_ref[...]
        h = h * (1.0 + mod_ref[bs, 1:2, :]) + mod_ref[bs, 0:1, :]
        hb = h.astype(BF16).reshape(nb * LANES, D_MODEL)
        ls = slice(s * nb * LANES, (s + 1) * nb * LANES)
        au = _dot_nt(wt_ref[0:A_WIDTH, :], hb)
        for k in range(nb):
            au_ref[0, :, s * nb + k, :] = au[:, k * LANES:(k + 1) * LANES]
        for r in range(0, CT_ROWS, rows):
            pt_ref[0, r:r + rows, ls] = _dot_nt(wt_ref[A_WIDTH + r:A_WIDTH + r + rows, :], hb).astype(BF16)
        for c0 in range(0, NAT_COLS, NAT_PIECE):
            sl = slice(c0, min(c0 + NAT_PIECE, NAT_COLS))
            c_ref[bs, :, sl] = _dot(hb, wc_ref[:, sl]).astype(BF16).reshape(nb, LANES, sl.stop - sl.start)


def _proj(x, mod, norm_w, wt, wc):
    bsz, L, d = x.shape
    nchunk = L // LANES
    return pl.pallas_call(
        _proj_kernel,
        grid=(nchunk,),
        in_specs=[pl.BlockSpec((bsz, LANES, d), lambda i: (0, i, 0)),
                  _resident((bsz, 3, d), lambda i: (0, 0, 0)),
                  _resident((1, d), lambda i: (0, 0)),
                  _resident(wt.shape, lambda i: (0, 0)),
                  _resident(wc.shape, lambda i: (0, 0))],
        out_specs=[pl.BlockSpec((1, A_WIDTH, bsz, LANES), lambda i: (i, 0, 0, 0)),
                   pl.BlockSpec((1, CT_ROWS, bsz * LANES), lambda i: (i, 0, 0)),
                   pl.BlockSpec((bsz, LANES, NAT_COLS), lambda i: (0, i, 0))],
        out_shape=[jax.ShapeDtypeStruct((nchunk, A_WIDTH, bsz, LANES), F32),
                   jax.ShapeDtypeStruct((nchunk, CT_ROWS, bsz * LANES), BF16),
                   jax.ShapeDtypeStruct((bsz, L, NAT_COLS), BF16)],
        compiler_params=_params("arbitrary"),
        name="proj",
    )(x, mod, norm_w, wt, wc)


def _s5_consts(lam_re, lam_im, log_dt, b_re, b_im, c_re, c_im):
    T = LANES
    G, P, H = A_GROUPS, A_STATE, A_GROUP_CH
    dt = jnp.exp(log_dt.astype(F32))[..., None]
    lr, li = lam_re.astype(F32) * dt, lam_im.astype(F32) * dt

    def power(k):
        mag = jnp.exp(lr * k)
        return mag * jnp.cos(li * k), mag * jnp.sin(li * k)

    def cmul(ar, ai, br, bi):
        return ar * br - ai * bi, ar * bi + ai * br

    p1r, p1i = power(1.0)
    den = lam_re.astype(F32) ** 2 + lam_im.astype(F32) ** 2
    cfr, cfi = cmul(p1r - 1.0, p1i, lam_re.astype(F32) / den, -lam_im.astype(F32) / den)
    bbr, bbi = cmul(cfr[..., None], cfi[..., None], b_re.astype(F32)[None], b_im.astype(F32)[None])
    cr, ci = c_re.astype(F32), c_im.astype(F32)

    k = jnp.arange(T + 1, dtype=F32)[:, None, None, None]
    pwr, pwi = power(k)
    cbr, cbi = cmul(cr[..., None], ci[..., None], bbr[:, :, None], bbi[:, :, None])
    kern = (jnp.einsum("kdgp,dghpi->dkghi", pwr[:T], cbr, precision=HIGHEST)
            - jnp.einsum("kdgp,dghpi->dkghi", pwi[:T], cbi, precision=HIGHEST))
    two = jnp.concatenate([kern[1][:0:-1], (kern[0][0] + kern[1][0])[None], kern[0][1:],
                           jnp.zeros((1,) + kern.shape[2:], F32)], axis=0)
    k2 = two.transpose(1, 3, 2, 0).reshape(G, H * H, 2 * T)

    sf_r, sf_i = cmul(pwr[:T][::-1, 0, :, :, None], pwi[:T][::-1, 0, :, :, None], bbr[0][None], bbi[0][None])
    sb_r, sb_i = cmul(pwr[:T, 1, :, :, None], pwi[:T, 1, :, :, None], bbr[1][None], bbi[1][None])
    bmat = jnp.concatenate([sf_r, sb_r, sf_i, sb_i], axis=2)
    bmat = bmat.transpose(1, 3, 0, 2).reshape(G, H * T, 4 * P)

    mf_r, mf_i = cmul(cr[0][None], ci[0][None], pwr[1:, 0, :, None, :], pwi[1:, 0, :, None, :])
    mb_r, mb_i = cmul(cr[1][None], ci[1][None], pwr[1:][::-1, 1, :, None, :], pwi[1:][::-1, 1, :, None, :])
    zero = jnp.zeros_like(mf_r)
    cf = jnp.concatenate([mf_r, zero, -mf_i, zero], axis=3)
    cb = jnp.concatenate([zero, mb_r, zero, -mb_i], axis=3)
    cf = cf.transpose(1, 3, 2, 0).reshape(G, 4 * P, H * T)
    cb = cb.transpose(1, 3, 2, 0).reshape(G, 4 * P, H * T)

    lam_t = jnp.stack([jnp.concatenate([pwr[T, 0], pwr[T, 1]], axis=-1),
                       jnp.concatenate([pwi[T, 0], pwi[T, 1]], axis=-1)], axis=1)
    return k2, bmat.astype(BF16), cf.astype(BF16), cb.astype(BF16), lam_t


def _s5_kernel(*refs, ntrunk):
    u_refs = refs[:ntrunk]
    k2_ref, b_ref, cf_ref, cb_ref, lam_ref, d_ref = refs[ntrunk:ntrunk + 6]
    y_refs = refs[ntrunk + 6:2 * ntrunk + 6]
    tt, a_scr, sr_scr, si_scr, xf_scr, xb_scr = refs[2 * ntrunk + 6:]
    H, T = A_GROUP_CH, LANES

    def toeplitz(hp, carry):
        rows = k2_ref[0, pl.ds(pl.multiple_of(hp * H, H), H), :]
        for h in range(H):
            x = jnp.broadcast_to(rows[h:h + 1, :], (T, 2 * T))
            r = pltpu.roll(x, 1, axis=1, stride=1, stride_axis=0)
            tt[pl.ds(pl.multiple_of(hp * T, T), T), h * T:(h + 1) * T] = r[:, T:].astype(BF16)
        return carry

    lax.fori_loop(0, H, toeplitz, 0)

    lam_r = jnp.broadcast_to(lam_ref[0, 0:1, :], (SUBLANES, LANES))
    lam_i = jnp.broadcast_to(lam_ref[0, 1:2, :], (SUBLANES, LANES))
    fwd_lane = lax.broadcasted_iota(jnp.int32, (SUBLANES, LANES), 1) < A_STATE
    zero = jnp.zeros((SUBLANES, LANES), F32)

    for u_ref, y_ref in zip(u_refs, y_refs):
        nchunk = u_ref.shape[0]
        m = nchunk * SUBLANES
        for hp in range(H):
            a_scr[0:m, hp * T:(hp + 1) * T] = u_ref[:, hp].reshape(m, T).astype(BF16)
        s = _dot(a_scr[0:m, :], b_ref[0])
        sr_scr[0:m, :] = s[:, :LANES]
        si_scr[0:m, :] = s[:, LANES:]

        def step(i, carry, nchunk=nchunk):
            xr, xi = carry
            rf = pl.ds(pl.multiple_of(i * SUBLANES, SUBLANES), SUBLANES)
            rb = pl.ds(pl.multiple_of((nchunk - 1 - i) * SUBLANES, SUBLANES), SUBLANES)
            xf_scr[rf, 0:LANES] = xr
            xf_scr[rf, LANES:2 * LANES] = xi
            xb_scr[rb, 0:LANES] = xr
            xb_scr[rb, LANES:2 * LANES] = xi
            sr = jnp.where(fwd_lane, sr_scr[rf, :], sr_scr[rb, :])
            si = jnp.where(fwd_lane, si_scr[rf, :], si_scr[rb, :])
            return lam_r * xr - lam_i * xi + sr, lam_r * xi + lam_i * xr + si

        lax.fori_loop(0, nchunk, step, (zero, zero))

        a = a_scr[0:m, :]
        xf = xf_scr[0:m, :].astype(BF16)
        xb = xb_scr[0:m, :].astype(BF16)
        for j in range(H // 2):
            sl = slice(j * 2 * T, (j + 1) * 2 * T)
            yj = _dot(a, tt[:, sl]) + _dot(xf, cf_ref[0, :, sl]) + _dot(xb, cb_ref[0, :, sl])
            for k in range(2):
                h = 2 * j + k
                y_ref[:, h] = yj[:, k * T:(k + 1) * T].reshape(nchunk, SUBLANES, T) + u_ref[:, h] * d_ref[h]


def _s5(au4s, consts, d4):
    k2, bmat, cf, cb, lam_t = consts
    bsz = au4s[0].shape[2]
    assert bsz == SUBLANES
    G, H, T = A_GROUPS, A_GROUP_CH, LANES
    m = max(a.shape[0] for a in au4s) * bsz
    per_g = lambda g: (g, 0, 0)
    data = lambda a: pl.BlockSpec((a.shape[0], H, bsz, T), lambda g: (0, g, 0, 0))
    return pl.pallas_call(
        functools.partial(_s5_kernel, ntrunk=len(au4s)),
        grid=(G,),
        in_specs=[data(a) for a in au4s] + [
            pl.BlockSpec((1, H * H, 2 * T), per_g),
            pl.BlockSpec((1, H * T, 4 * A_STATE), per_g),
            pl.BlockSpec((1, 4 * A_STATE, H * T), per_g),
            pl.BlockSpec((1, 4 * A_STATE, H * T), per_g),
            pl.BlockSpec((1, 2, LANES), per_g),
            pl.BlockSpec((H, bsz, T), per_g)],
        out_specs=[data(a) for a in au4s],
        out_shape=[jax.ShapeDtypeStruct(a.shape, F32) for a in au4s],
        scratch_shapes=[pltpu.VMEM((H * T, H * T), BF16), pltpu.VMEM((m, H * T), BF16),
                        pltpu.VMEM((m, LANES), F32), pltpu.VMEM((m, LANES), F32),
                        pltpu.VMEM((m, 2 * LANES), F32), pltpu.VMEM((m, 2 * LANES), F32)],
        compiler_params=_params("arbitrary"),
        name="s5",
    )(*au4s, k2, bmat, cf, cb, lam_t, d4)


def _dft_tables(L):
    N = 2 * L
    n2 = LANES
    n1 = N // n2
    h = n1 // 2
    k = np.arange(n1)
    f1 = np.exp(-2j * np.pi * np.outer(k, k) / n1)
    f2 = np.exp(-2j * np.pi * np.outer(np.arange(n2), np.arange(n2)) / n2)
    tw = np.exp(-2j * np.pi * np.outer(k, np.arange(n2)) / N)
    f1m = np.block([[f1.real[:, :h], -f1.imag[:, :h]], [f1.imag[:, :h], f1.real[:, :h]]])
    ffm = np.concatenate([f1.real, f1.imag], axis=0)
    f2m = np.block([[f2.real, f2.imag], [-f2.imag, f2.real]])
    g2m = np.block([[f2.real, -f2.imag], [f2.imag, f2.real]])
    g1 = np.conj(f1)[:h, :]
    g1m = np.block([[g1.real, -g1.imag], [g1.imag, g1.real]])
    return dict(n1=n1, f1m=f1m, ffm=ffm, f2m=f2m, g2m=g2m, g1m=g1m, twr=tw.real, twi=tw.imag)


def _filter_positions(L):
    N = 2 * L
    i = np.arange(N)
    pos = np.where(i < L, i, N - i).astype(np.float32)
    t = pos / np.float32(max(L - 1, 1))
    w = np.float32(2.0 * math.pi) * pos / np.float32(L)
    bands = np.linspace(1e-4, HY_BANDS - 1, HY_BANDS, dtype=np.float32)
    wb = (w[:, None] * bands).astype(np.float32)
    feats = np.concatenate([t[:, None], np.cos(wb), -np.sin(wb)], axis=-1).astype(np.float32)
    fpad = np.zeros((LANES, N), np.float32)
    fpad[:HY_POS_FEAT] = feats.T
    keep = np.ones((N,), np.float32)
    keep[L] = 0.0
    n1 = N // LANES
    return fpad, t.reshape(n1, 1, LANES), keep.reshape(n1, 1, LANES)


def _filt_kernel(feat_ref, w1t_ref, b1_ref, w2t_ref, b2_ref, w3t_ref, sf_ref, ld_ref, t_ref, keep_ref,
                 ts_ref, scale_ref, ss_scr, dec_scr, *, inv_n, half_steps):
    i = pl.program_id(0)

    @pl.when(i % half_steps == 0)
    def _():
        dec_scr[...] = jnp.exp(ld_ref[0])

    @pl.when(i == 0)
    def _():
        ss_scr[...] = jnp.zeros_like(ss_scr)

    h1 = jnp.sin(sf_ref[:, 0:1] * (_dot(w1t_ref[...], feat_ref[...], precision=HIGHEST) + b1_ref[...]))
    h2 = jnp.sin(sf_ref[:, 1:2] * (_dot(w2t_ref[...], h1, precision=HIGHEST) + b2_ref[...]))
    w_hi, w_lo = _split_bf16(w3t_ref[0])
    h_hi, h_lo = _split_bf16(h2)
    ss = ss_scr[...]
    for k in range(FILT_SEG):
        sl = slice(k * LANES, (k + 1) * LANES)
        h3t = _dot(w_hi, h_hi[:, sl]) + _dot(w_hi, h_lo[:, sl]) + _dot(w_lo, h_hi[:, sl])
        win = jnp.exp(-dec_scr[...] * t_ref[k])
        ts = h3t * win * keep_ref[k]
        ts_ref[k] = ts.astype(BF16)
        ss = ss + ts * ts
    ss_scr[...] = ss

    @pl.when(i == pl.num_programs(0) - 1)
    def _():
        tot = jnp.sum(ss_scr[...], axis=-1, keepdims=True)
        scale_ref[...] = lax.rsqrt(tot + EPS) * inv_n


def _filt_fft_kernel(ts_ref, ffm_ref, twr_ref, twi_ref, f2m_ref, hr_ref, hi_ref, *, n1, cb):
    tr, ti = twr_ref[...], twi_ref[...]
    group = 8
    for g in range(cb // group):
        a = _dot(ffm_ref[...], ts_ref[:, g * group * LANES:(g + 1) * group * LANES])
        for c in range(group):
            sl = slice(c * LANES, (c + 1) * LANES)
            ar, ai = a[:n1, sl], a[n1:, sl]
            cat = jnp.concatenate([ar * tr - ai * ti, ar * ti + ai * tr], axis=1).astype(BF16)
            x = _dot(cat, f2m_ref[...]).astype(BF16)
            out = slice((g * group + c) * LANES, (g * group + c + 1) * LANES)
            hr_ref[:, out] = x[:, :LANES]
            hi_ref[:, out] = x[:, LANES:]


def _hyena_filters(L, w1, b1, w2, b2, w3, sin_freq, log_decay, cb=32):
    tabs = _dft_tables(L)
    n1 = tabs["n1"]
    N = 2 * L
    nch = HY_ORDER * B_WIDTH
    fpad, tpos, keep = _filter_positions(L)
    w1t = jnp.zeros((HY_FILT_HID, LANES), F32).at[:, :HY_POS_FEAT].set(w1.astype(F32).T)
    w3t = w3.astype(F32).T.reshape(2, nch, HY_FILT_HID)
    ld = log_decay.astype(F32).reshape(2, nch, 1)
    half = n1 // 2 // FILT_SEG
    col = lambda v: v.astype(F32).reshape(-1, 1)
    fixed = lambda shape: pl.BlockSpec(shape, lambda i: (0,) * len(shape))
    ts, scale = pl.pallas_call(
        functools.partial(_filt_kernel, inv_n=1.0 / N, half_steps=half),
        grid=(n1 // FILT_SEG,),
        in_specs=[pl.BlockSpec((LANES, FILT_SEG * LANES), lambda i: (0, i)),
                  fixed((HY_FILT_HID, LANES)), fixed((HY_FILT_HID, 1)),
                  fixed((HY_FILT_HID, HY_FILT_HID)), fixed((HY_FILT_HID, 1)),
                  pl.BlockSpec((1, nch, HY_FILT_HID), lambda i: (i // half, 0, 0)),
                  fixed((HY_FILT_HID, 2)),
                  pl.BlockSpec((1, nch, 1), lambda i: (i // half, 0, 0)),
                  pl.BlockSpec((FILT_SEG, 1, LANES), lambda i: (i, 0, 0)),
                  pl.BlockSpec((FILT_SEG, 1, LANES), lambda i: (i, 0, 0))],
        out_specs=[pl.BlockSpec((FILT_SEG, nch, LANES), lambda i: (i, 0, 0)),
                   pl.BlockSpec((nch, 1), lambda i: (0, 0))],
        out_shape=[jax.ShapeDtypeStruct((n1, nch, LANES), BF16),
                   jax.ShapeDtypeStruct((nch, 1), F32)],
        scratch_shapes=[pltpu.VMEM((nch, LANES), F32), pltpu.VMEM((nch, 1), F32)],
        compiler_params=_params("arbitrary"),
        name="hy_filter",
    )(jnp.asarray(fpad), w1t, col(b1), w2.astype(F32).T, col(b2), w3t, sin_freq.astype(F32).T, ld,
      jnp.asarray(tpos), jnp.asarray(keep))
    W = cb * LANES
    hr, hi = pl.pallas_call(
        functools.partial(_filt_fft_kernel, n1=n1, cb=cb),
        grid=(nch // cb,),
        in_specs=[pl.BlockSpec((n1, W), lambda j: (0, j)),
                  pl.BlockSpec((2 * n1, n1), lambda j: (0, 0)),
                  pl.BlockSpec((n1, LANES), lambda j: (0, 0)),
                  pl.BlockSpec((n1, LANES), lambda j: (0, 0)),
                  pl.BlockSpec((2 * LANES, 2 * LANES), lambda j: (0, 0))],
        out_specs=[pl.BlockSpec((n1, W), lambda j: (0, j)),
                   pl.BlockSpec((n1, W), lambda j: (0, j))],
        out_shape=[jax.ShapeDtypeStruct((n1, nch * LANES), BF16),
                   jax.ShapeDtypeStruct((n1, nch * LANES), BF16)],
        compiler_params=_params("arbitrary"),
        name="hy_filter_fft",
    )(ts.reshape(n1, nch * LANES), jnp.asarray(tabs["ffm"], F32).astype(BF16), jnp.asarray(tabs["twr"], F32),
      jnp.asarray(tabs["twi"], F32), jnp.asarray(tabs["f2m"], F32).astype(BF16))
    scale_rep = jnp.repeat(scale.reshape(HY_ORDER, B_WIDTH), LANES, axis=1)
    return hr, hi, scale_rep


def _shift_time(x, direction):
    rows = x.shape[0]
    lane = lax.broadcasted_iota(jnp.int32, x.shape, 1)
    row = lax.broadcasted_iota(jnp.int32, x.shape, 0)
    top = row[:SUBLANES]
    if direction > 0:
        r = pltpu.roll(x, 1, axis=1)
        e = pltpu.roll(r, 1, axis=0)
        edge = jnp.concatenate([jnp.where(top == 0, 0.0, e[:SUBLANES]), e[SUBLANES:]], axis=0)
        return jnp.where(lane == 0, edge, r)
    r = pltpu.roll(x, LANES - 1, axis=1)
    e = pltpu.roll(r, rows - 1, axis=0)
    edge = jnp.concatenate([e[:rows - SUBLANES], jnp.where(top == SUBLANES - 1, 0.0, e[rows - SUBLANES:])], axis=0)
    return jnp.where(lane == LANES - 1, edge, r)


def _hyena_kernel(g0_ref, g1_ref, z_ref, bg_ref, cw0_ref, cw1_ref, cwz_ref, skip_ref, scale_ref,
                  h0r_ref, h0i_ref, h1r_ref, h1i_ref, f1m_ref, g1m_ref, f2m_ref, g2m_ref, twr_ref, twi_ref,
                  o_ref, z_scr, g0_scr, g1_scr, a_scr, b_scr, *, n1, cb, bsz):
    nh = n1 // 2
    npair = bsz // 2
    tile = lambda i: slice(i * LANES, (i + 1) * LANES)

    def short_conv(src, cw_ref, dst):
        for c in range(cb):
            w0, w1, w2, bias = (cw_ref[k:k + 1, tile(c)] for k in range(4))
            for b in range(bsz):
                x = src[:, tile(c * bsz + b)].astype(F32)
                y = bias + w0 * _shift_time(x, 1) + w1 * x + w2 * _shift_time(x, -1)
                half, p = divmod(b, npair)
                dst[half * nh:(half + 1) * nh, tile(c * npair + p)] = y

    short_conv(g0_ref, cw0_ref, g0_scr)
    short_conv(g1_ref, cw1_ref, g1_scr)
    short_conv(z_ref, cwz_ref, z_scr)
    tr, ti = twr_ref[...].astype(BF16), twi_ref[...].astype(BF16)

    def long_conv(hr_ref, hi_ref):
        a_scr[...] = _dot(f1m_ref[...], z_scr[...].astype(BF16)).astype(BF16)
        for c in range(cb):
            hr, hi = hr_ref[:, tile(c)], hi_ref[:, tile(c)]
            for p in range(npair):
                sl = tile(c * npair + p)
                ar, ai = a_scr[0:n1, sl], a_scr[n1:2 * n1, sl]
                x = _dot(jnp.concatenate([ar * tr - ai * ti, ar * ti + ai * tr], axis=1), f2m_ref[...]).astype(BF16)
                xr, xi = x[:, :LANES], x[:, LANES:]
                y = _dot(jnp.concatenate([xr * hr - xi * hi, xr * hi + xi * hr], axis=1), g2m_ref[...]).astype(BF16)
                br, bi = y[:, :LANES], y[:, LANES:]
                b_scr[0:n1, sl] = br * tr + bi * ti
                b_scr[n1:2 * n1, sl] = bi * tr - br * ti
        return _dot(g1m_ref[...], b_scr[...])

    def per_channel(ref, row):
        return jnp.concatenate([ref[row:row + 1, tile(c)] for c in range(cb) for _ in range(npair)], axis=1)

    z0 = z_scr[...]
    z1 = g0_scr[...] * (long_conv(h0r_ref, h0i_ref) * per_channel(scale_ref, 0) + z0 * per_channel(skip_ref, 0))
    z_scr[...] = z1
    z2 = g1_scr[...] * (long_conv(h1r_ref, h1i_ref) * per_channel(scale_ref, 1) + z1 * per_channel(skip_ref, 1))
    z_scr[...] = z2
    for c in range(cb):
        for b in range(bsz):
            half, p = divmod(b, npair)
            gate = _silu(bg_ref[:, tile(c * bsz + b)].astype(F32))
            o_ref[:, tile(c * bsz + b)] = (z_scr[half * nh:(half + 1) * nh, tile(c * npair + p)] * gate).astype(BF16)


def _hyena(pt2, filt, conv_w, conv_b, skip, L, bsz, cb=8):
    hr, hi, scale_rep = filt
    tabs = _dft_tables(L)
    n1 = tabs["n1"]
    nh = n1 // 2
    assert nh % SUBLANES == 0, "sequence length must be a multiple of 1024"
    wd = cb * bsz * LANES
    wc = cb * LANES
    ws = cb * (bsz // 2) * LANES
    ncb = B_WIDTH // cb
    cw = jnp.repeat(jnp.concatenate([conv_w.astype(F32), conv_b.astype(F32)[None]], axis=0), LANES, axis=1)
    sk = jnp.repeat(skip.astype(F32), LANES, axis=1)
    data = lambda row0: pl.BlockSpec((nh, wd), lambda j: (0, row0 // cb + j))
    cws = lambda grp: pl.BlockSpec((4, wc), lambda j: (0, grp * ncb + j))
    hs = lambda order: pl.BlockSpec((n1, wc), lambda j: (0, order * ncb + j))
    const = lambda j: (0, 0)
    return pl.pallas_call(
        functools.partial(_hyena_kernel, n1=n1, cb=cb, bsz=bsz),
        grid=(ncb,),
        in_specs=[data(CT_BU), data(CT_BU + B_WIDTH), data(CT_BU + 2 * B_WIDTH), data(CT_BG),
                  cws(0), cws(1), cws(2),
                  pl.BlockSpec((2, wc), lambda j: (0, j)), pl.BlockSpec((2, wc), lambda j: (0, j)),
                  hs(0), hs(0), hs(1), hs(1),
                  pl.BlockSpec((2 * n1, n1), const), pl.BlockSpec((n1, 2 * n1), const),
                  pl.BlockSpec((2 * LANES, 2 * LANES), const), pl.BlockSpec((2 * LANES, 2 * LANES), const),
                  pl.BlockSpec((n1, LANES), const), pl.BlockSpec((n1, LANES), const)],
        out_specs=pl.BlockSpec((nh, wd), lambda j: (0, j)),
        out_shape=jax.ShapeDtypeStruct((nh, B_WIDTH * bsz * LANES), BF16),
        scratch_shapes=[pltpu.VMEM((n1, ws), F32), pltpu.VMEM((n1, ws), F32), pltpu.VMEM((n1, ws), F32),
                        pltpu.VMEM((2 * n1, ws), BF16), pltpu.VMEM((2 * n1, ws), BF16)],
        compiler_params=_params("arbitrary"),
        name="hyena",
    )(pt2, pt2, pt2, pt2, cw, cw, cw, sk, scale_rep, hr, hi, hr, hi,
      jnp.asarray(tabs["f1m"], F32).astype(BF16), jnp.asarray(tabs["g1m"], F32).astype(BF16),
      jnp.asarray(tabs["f2m"], F32).astype(BF16), jnp.asarray(tabs["g2m"], F32).astype(BF16),
      jnp.asarray(tabs["twr"], F32), jnp.asarray(tabs["twi"], F32))


def _gla_prep(q, z, v, lb, tri, forward):
    C = GLA_CHUNK
    half, quarter = C // 2, C // 4
    one_m = 1.0 - lb
    sig = _sigmoid(z)
    f = lb + one_m * sig
    logf = jnp.log2(jnp.maximum(f, F_FLOOR))
    kk = one_m * (1.0 - sig)
    l_hi, l_lo = _split_bf16(logf)
    acc = _dot(tri, l_hi) + _dot(tri, l_lo)
    if forward:
        tot = acc[C - 1:C]
        a_low, a_high, a_cross = acc[quarter - 1:quarter], acc[half + quarter - 1:half + quarter], acc[half - 1:half]
    else:
        tot = acc[0:1]
        a_low, a_high, a_cross = acc[quarter:quarter + 1], acc[half + quarter:half + quarter + 1], acc[half:half + 1]
    e_leaf = jnp.concatenate([acc[:half] - a_low, acc[half:] - a_high], axis=0)
    ql = (q * jnp.exp2(e_leaf)).astype(BF16)
    kl = (kk * jnp.exp2(-e_leaf)).astype(BF16)
    zeros = jnp.zeros((half, q.shape[1]), BF16)
    if forward:
        qc = jnp.concatenate([zeros, (q[half:] * jnp.exp2(acc[half:] - a_cross)).astype(BF16)], axis=0)
        kc = jnp.concatenate([(kk[:half] * jnp.exp2(a_cross - acc[:half])).astype(BF16), zeros], axis=0)
    else:
        qc = jnp.concatenate([(q[:half] * jnp.exp2(acc[:half] - a_cross)).astype(BF16), zeros], axis=0)
        kc = jnp.concatenate([zeros, (kk[half:] * jnp.exp2(a_cross - acc[half:])).astype(BF16)], axis=0)
    qi = (q * jnp.exp2(acc)).astype(BF16)
    ks = (kk * jnp.exp2(tot - acc)).astype(BF16)
    return ql, kl, qc, kc, qi, ks, jnp.exp2(tot), v.astype(BF16)


def _gla_matmuls(ops, st_ref, o_ref, rows, forward, p):
    ql, kl, qc, kc, qi, ks, dec, vb = ops
    C = GLA_CHUNK
    half = C // 2
    ti = lax.broadcasted_iota(jnp.int32, (2 * C, C), 0) % C
    si = lax.broadcasted_iota(jnp.int32, (2 * C, C), 1)
    causal = (ti >= si) if forward else (si >= ti)
    leaf_mask = jnp.logical_and(causal, (ti < half) == (si < half))
    first = lax.broadcasted_iota(jnp.int32, (C, LANES), 1) < C_HEAD_DIM
    head0 = jnp.where(first, 1.0, 0.0).astype(BF16)
    head1 = jnp.where(first, 0.0, 1.0).astype(BF16)
    r2 = lax.broadcasted_iota(jnp.int32, (LANES, LANES), 0) < C_HEAD_DIM
    c2 = lax.broadcasted_iota(jnp.int32, (LANES, LANES), 1) < C_HEAD_DIM
    same_head = r2 == c2

    def by_head(x):
        return jnp.concatenate([x * head0, x * head1], axis=0)

    s = (jnp.where(leaf_mask, _dot_nt(by_head(ql), kl), 0.0) + _dot_nt(by_head(qc), kc)).astype(BF16)
    intra = _dot(jnp.concatenate([s[:C], s[C:]], axis=1), by_head(vb))
    st = st_ref[p]
    inter = _dot_nt(qi, st.astype(BF16))
    o_ref[0, rows, p * LANES:(p + 1) * LANES] = (intra + inter).astype(o_ref.dtype)
    upd = _dot_tn(vb, ks)
    st_ref[p] = st * dec + jnp.where(same_head, upd, 0.0)


def _hgrn_kernel(qf_ref, zf_ref, vf_ref, qb_ref, zb_ref, vb_ref, lb_ref, tri_ref, of_ref, ob_ref, stf, stb, *, nsub):
    @pl.when(pl.program_id(1) == 0)
    def _():
        stf[...] = jnp.zeros_like(stf)
        stb[...] = jnp.zeros_like(stb)

    items = []
    for j in range(nsub):
        for p in range(C_WIDTH // LANES):
            items.append((qf_ref, zf_ref, vf_ref, of_ref, stf, 0, slice(j * GLA_CHUNK, (j + 1) * GLA_CHUNK), True, p))
            items.append((qb_ref, zb_ref, vb_ref, ob_ref, stb, 1,
                          slice((nsub - 1 - j) * GLA_CHUNK, (nsub - j) * GLA_CHUNK), False, p))

    def prep(item):
        q_ref, z_ref, v_ref, _, _, d, rows, forward, p = item
        sl = slice(p * LANES, (p + 1) * LANES)
        return _gla_prep(q_ref[0, rows, sl].astype(F32), z_ref[0, rows, sl].astype(F32),
                         v_ref[0, rows, sl].astype(F32), lb_ref[d:d + 1, sl], tri_ref[d], forward)

    ops = prep(items[0])
    for i, item in enumerate(items):
        nxt = prep(items[i + 1]) if i + 1 < len(items) else None
        _gla_matmuls(ops, item[4], item[3], item[6], item[7], item[8])
        ops = nxt


def _hgrn(c_all, lb, rb):
    bsz, L, _ = c_all.shape
    nblk = L // rb
    w = C_WIDTH
    fwd = lambda col: pl.BlockSpec((1, rb, w), lambda b, i: (b, i, col))
    bwd = lambda col: pl.BlockSpec((1, rb, w), lambda b, i: (b, nblk - 1 - i, col))
    npair = w // LANES
    idx = np.arange(GLA_CHUNK)
    tri = jnp.asarray(np.stack([idx[:, None] >= idx[None, :], idx[None, :] >= idx[:, None]]), BF16)
    return pl.pallas_call(
        functools.partial(_hgrn_kernel, nsub=rb // GLA_CHUNK),
        grid=(bsz, nblk),
        in_specs=[fwd(0), fwd(1), fwd(3), bwd(0), bwd(2), bwd(3), pl.BlockSpec((2, w), lambda b, i: (0, 0)),
                  pl.BlockSpec((2, GLA_CHUNK, GLA_CHUNK), lambda b, i: (0, 0, 0))],
        out_specs=[pl.BlockSpec((1, rb, w), lambda b, i: (b, i, 0)),
                   pl.BlockSpec((1, rb, w), lambda b, i: (b, nblk - 1 - i, 0))],
        out_shape=[jax.ShapeDtypeStruct((bsz, L, w), BF16), jax.ShapeDtypeStruct((bsz, L, w), BF16)],
        scratch_shapes=[pltpu.VMEM((npair, LANES, LANES), F32), pltpu.VMEM((npair, LANES, LANES), F32)],
        compiler_params=_params("arbitrary", "arbitrary"),
        name="hgrn",
    )(c_all, c_all, c_all, c_all, c_all, c_all, lb, tri)


def _gelu_tanh(x):
    return 0.5 * x * (1.0 + jnp.tanh(math.sqrt(2.0 / math.pi) * (x + 0.044715 * (x * x * x))))


def _out_kernel(x_ref, mod_ref, ys_ref, ag_ref, hy_ref, of_ref, ob_ref, cg_ref, gw_ref, gb_ref,
                hn_ref, ones_ref, wo_ref, fn_ref, o_ref, *, final):
    bsz = x_ref.shape[0]
    m = bsz * LANES
    y_rows, g_rows, b_rows = [], [], []
    for b in range(bsz):
        lanes = slice(b * LANES, (b + 1) * LANES)
        y_rows.append(ys_ref[0, :, b, :].T)
        g_rows.append(ag_ref[0, :, lanes].astype(F32).T)
        b_rows.append(hy_ref[0, :, lanes].astype(F32).T)
    z = _gelu_tanh(jnp.concatenate(y_rows, axis=0))
    glu = _sigmoid(_dot(z.astype(BF16), gw_ref[...]) + gb_ref[...])
    a_out = (z * glu * _silu(jnp.concatenate(g_rows, axis=0))).astype(BF16)
    b_out = jnp.concatenate(b_rows, axis=0).astype(BF16)

    o = (of_ref[...].astype(F32) + ob_ref[...].astype(F32)).reshape(m, C_WIDTH)
    ms = _dot((o * o).astype(BF16), ones_ref[...]) * (1.0 / C_HEAD_DIM)
    gate_c = _silu(cg_ref[...].astype(F32)).reshape(m, C_WIDTH)
    c_out = (o * lax.rsqrt(ms + EPS) * hn_ref[...] * gate_c).astype(BF16)

    mixed = _dot(jnp.concatenate([a_out, b_out, c_out], axis=1), wo_ref[...])
    xn = x_ref[...] + mod_ref[:, 2:3, :] * mixed.reshape(bsz, LANES, D_MODEL)
    if final:
        msq = jnp.mean(xn * xn, axis=-1, keepdims=True)
        xn = xn * lax.rsqrt(msq + EPS) * fn_ref[...]
    o_ref[...] = xn


def _out(x, mod, ys4, pt3, hy3, of, ob, c_all, gw, gb, hn, ones_bd, wo, fn, final):
    bsz, L, dm = x.shape
    row = lambda w, col: pl.BlockSpec((bsz, LANES, w), lambda i: (0, i, col))
    full = lambda a: _resident(a.shape, lambda i: (0,) * a.ndim)
    return pl.pallas_call(
        functools.partial(_out_kernel, final=final),
        grid=(L // LANES,),
        in_specs=[row(dm, 0), full(mod),
                  pl.BlockSpec((1, A_WIDTH, bsz, LANES), lambda i: (i, 0, 0, 0)),
                  pl.BlockSpec((1, A_WIDTH, bsz * LANES), lambda i: (i, CT_AG // A_WIDTH, 0)),
                  pl.BlockSpec((1, B_WIDTH, bsz * LANES), lambda i: (i, 0, 0)),
                  row(C_WIDTH, 0), row(C_WIDTH, 0), row(C_WIDTH, 4),
                  full(gw), full(gb), full(hn), full(ones_bd), full(wo), full(fn)],
        out_specs=row(dm, 0),
        out_shape=jax.ShapeDtypeStruct((bsz, L, dm), F32),
        compiler_params=_params("arbitrary"),
        name="out",
    )(x, mod, ys4, pt3, hy3, of, ob, c_all, gw, gb, hn, ones_bd, wo, fn)


def _layer(xs, mods, p, layer, hg_rb=2048):
    head = np.arange(C_WIDTH) // C_HEAD_DIM
    ones_bd = jnp.asarray(head[:, None] == head[None, :], BF16)
    w_in = p["w_in"][layer]
    o_b = 2 * A_WIDTH + 4 * B_WIDTH
    wt = w_in[:, :o_b].T.astype(BF16)
    wc = w_in[:, o_b:].astype(BF16)
    norm_w = p["norm_w"][layer].reshape(1, -1).astype(F32)
    projs = [_proj(x, mod, norm_w, wt, wc) for x, mod in zip(xs, mods)]
    bsz = xs[0].shape[0]
    d4 = jnp.broadcast_to(p["s5_d"][layer].astype(F32)[:, None, None], (A_WIDTH, bsz, LANES))
    ys4s = _s5([au for au, _, _ in projs], p["s5_consts"][layer], d4)
    out = []
    for x, mod, (au, pt, c_all), ys4 in zip(xs, mods, projs, ys4s):
        L = x.shape[1]
        nchunk = L // LANES
        hy2 = _hyena(pt.reshape(nchunk, CT_ROWS * bsz * LANES), p["hy_filters"][(layer, L)], p["hy_conv_w"][layer],
                     p["hy_conv_b"][layer], p["hy_skip"][layer], L, bsz)
        of, ob = _hgrn(c_all, p["lower_bounds"][layer], hg_rb)
        out.append(_out(x, mod, ys4, pt, hy2.reshape(nchunk, B_WIDTH, bsz * LANES), of, ob, c_all,
                        p["s5_glu_w"][layer].astype(BF16), p["s5_glu_b"][layer].reshape(1, -1).astype(F32),
                        jnp.tile(p["hg_norm_w"][layer].astype(F32), C_WIDTH // C_HEAD_DIM).reshape(1, -1),
                        ones_bd, p["w_out"][layer].astype(BF16), p["final_norm_w"].reshape(1, -1).astype(F32),
                        layer == DEPTH - 1))
    return out


def kernel(x_prompt, x_sample, c_prompt, c_sample, norm_w, ada_w, ada_b, w_in, w_out, s5_lambda_re, s5_lambda_im, s5_log_dt, s5_b_re, s5_b_im, s5_c_re, s5_c_im, s5_d, s5_glu_w, s5_glu_b, hy_conv_w, hy_conv_b, hy_w1, hy_b1, hy_w2, hy_b2, hy_w3, hy_sin_freq, hy_log_decay, hy_skip, hg_lb_logits, hg_norm_w, final_norm_w):
    bp = x_prompt.shape[0]
    ada = _ada(jnp.concatenate([c_prompt, c_sample], axis=0).astype(F32), ada_w.astype(F32), ada_b.astype(F32))
    lb_soft = jax.nn.softmax(hg_lb_logits.astype(F32), axis=0)
    lower_bounds = jnp.cumsum(lb_soft, axis=0) - lb_soft[0:1]
    lengths = sorted({x_prompt.shape[1], x_sample.shape[1]})
    p = dict(
        norm_w=norm_w, w_in=w_in, w_out=w_out, s5_d=s5_d, s5_glu_w=s5_glu_w, s5_glu_b=s5_glu_b,
        hy_conv_w=hy_conv_w, hy_conv_b=hy_conv_b, hy_skip=hy_skip, hg_norm_w=hg_norm_w, final_norm_w=final_norm_w,
        lower_bounds=lower_bounds,
        s5_consts=[_s5_consts(s5_lambda_re[l], s5_lambda_im[l], s5_log_dt[l], s5_b_re[l], s5_b_im[l],
                              s5_c_re[l], s5_c_im[l]) for l in range(DEPTH)],
        hy_filters={(l, L): _hyena_filters(L, hy_w1[l], hy_b1[l], hy_w2[l], hy_b2[l], hy_w3[l], hy_sin_freq[l],
                                           hy_log_decay[l])
                    for l in range(DEPTH) for L in lengths},
    )
    xs = [x_prompt, x_sample]
    for layer in range(DEPTH):
        mods = [ada[layer, :bp].reshape(bp, 3, D_MODEL),
                ada[layer, bp:].reshape(x_sample.shape[0], 3, D_MODEL)]
        xs = _layer(xs, mods, p, layer)
    return (xs[0], xs[1])
```

```python
import functools
import math

import numpy as np
import jax
import jax.numpy as jnp
from jax import lax
from jax.experimental import pallas as pl
from jax.experimental.pallas import tpu as pltpu

F32 = jnp.float32
BF16 = jnp.bfloat16
HIGHEST = lax.Precision.HIGHEST

D_MODEL = 1024
DEPTH = 2
A_WIDTH = 256
A_GROUP_CH = 16
A_GROUPS = 16
A_STATE = 64
B_WIDTH = 384
HY_ORDER = 2
HY_BANDS = 16
HY_POS_FEAT = 1 + 2 * HY_BANDS
HY_FILT_HID = 64
C_WIDTH = 384
C_HEAD_DIM = 64
EPS = 1e-6
F_FLOOR = 1e-30

LANES = 128
SUBLANES = 8
GLA_CHUNK = 128
FILT_SEG = 8
VMEM_LIMIT = 56 * 1024 * 1024

CT_AG, CT_BU, CT_BG = 0, A_WIDTH, A_WIDTH + 3 * B_WIDTH
CT_ROWS = A_WIDTH + 4 * B_WIDTH
NAT_COLS = 5 * C_WIDTH
NAT_PIECE = 768


def _dot(a, b, precision=None):
    return jnp.dot(a, b, preferred_element_type=F32, precision=precision)


def _dot_nt(a, b, precision=None):
    return lax.dot_general(a, b, (((1,), (1,)), ((), ())), preferred_element_type=F32, precision=precision)


def _dot_tn(a, b, precision=None):
    return lax.dot_general(a, b, (((0,), (0,)), ((), ())), preferred_element_type=F32, precision=precision)


def _split_bf16(a):
    hi = a.astype(BF16)
    return hi, (a - hi.astype(F32)).astype(BF16)


def _sigmoid(x):
    return 1.0 / (1.0 + jnp.exp(-x))


def _silu(x):
    return x * _sigmoid(x)


def _params(*sem):
    return pltpu.CompilerParams(dimension_semantics=sem, vmem_limit_bytes=VMEM_LIMIT)


def _resident(shape, index_map):
    return pl.BlockSpec(shape, index_map, pipeline_mode=pl.Buffered(1))


def _ada_kernel(c_ref, w_ref, b_ref, o_ref):
    cond = _silu(c_ref[...])
    o_ref[0] = _dot(cond, w_ref[0], precision=HIGHEST) + b_ref[0]


def _ada(c_all, ada_w, ada_b):
    nb = c_all.shape[0]
    d = D_MODEL
    return pl.pallas_call(
        _ada_kernel,
        grid=(DEPTH, 3),
        in_specs=[pl.BlockSpec((nb, d), lambda l, j: (0, 0)),
                  pl.BlockSpec((1, d, d), lambda l, j: (l, 0, j)),
                  pl.BlockSpec((1, 1, d), lambda l, j: (l, 0, j))],
        out_specs=pl.BlockSpec((1, nb, d), lambda l, j: (l, 0, j)),
        out_shape=jax.ShapeDtypeStruct((DEPTH, nb, 3 * d), F32),
        compiler_params=_params("arbitrary", "arbitrary"),
        name="ada",
    )(c_all, ada_w, ada_b.reshape(DEPTH, 1, 3 * d))


def _proj_kernel(x_ref, mod_ref, nw_ref, wt_ref, wc_ref, au_ref, pt_ref, c_ref):
    bsz = x_ref.shape[0]
    nb = 2
    rows = 256
    for s in range(bsz // nb):
        bs = slice(s * nb, (s + 1) * nb)
        x = x_ref[bs]
        ms = jnp.mean(x * x, axis=-1, keepdims=True)
        h = x * lax.rsqrt(ms + EPS) * nw_ref[...]
        h = h * (1.0 + mod_ref[bs, 1:2, :]) + mod_ref[bs, 0:1, :]
        hb = h.astype(BF16).reshape(nb * LANES, D_MODEL)
        ls = slice(s * nb * LANES, (s + 1) * nb * LANES)
        au = _dot_nt(wt_ref[0:A_WIDTH, :], hb)
        for k in range(nb):
            au_ref[0, :, s * nb + k, :] = au[:, k * LANES:(k + 1) * LANES]
        for r in range(0, CT_ROWS, rows):
            pt_ref[0, r:r + rows, ls] = _dot_nt(wt_ref[A_WIDTH + r:A_WIDTH + r + rows, :], hb).astype(BF16)
        for c0 in range(0, NAT_COLS, NAT_PIECE):
            sl = slice(c0, min(c0 + NAT_PIECE, NAT_COLS))
            c_ref[bs, :, sl] = _dot(hb, wc_ref[:, sl]).astype(BF16).reshape(nb, LANES, sl.stop - sl.start)


def _proj(x, mod, norm_w, wt, wc):
    bsz, L, d = x.shape
    nchunk = L // LANES
    return pl.pallas_call(
        _proj_kernel,
        grid=(nchunk,),
        in_specs=[pl.BlockSpec((bsz, LANES, d), lambda i: (0, i, 0)),
                  _resident((bsz, 3, d), lambda i: (0, 0, 0)),
                  _resident((1, d), lambda i: (0, 0)),
                  _resident(wt.shape, lambda i: (0, 0)),
                  _resident(wc.shape, lambda i: (0, 0))],
        out_specs=[pl.BlockSpec((1, A_WIDTH, bsz, LANES), lambda i: (i, 0, 0, 0)),
                   pl.BlockSpec((1, CT_ROWS, bsz * LANES), lambda i: (i, 0, 0)),
                   pl.BlockSpec((bsz, LANES, NAT_COLS), lambda i: (0, i, 0))],
        out_shape=[jax.ShapeDtypeStruct((nchunk, A_WIDTH, bsz, LANES), F32),
                   jax.ShapeDtypeStruct((nchunk, CT_ROWS, bsz * LANES), BF16),
                   jax.ShapeDtypeStruct((bsz, L, NAT_COLS), BF16)],
        compiler_params=_params("arbitrary"),
        name="proj",
    )(x, mod, norm_w, wt, wc)


def _s5_consts(lam_re, lam_im, log_dt, b_re, b_im, c_re, c_im):
    T = LANES
    G, P, H = A_GROUPS, A_STATE, A_GROUP_CH
    dt = jnp.exp(log_dt.astype(F32))[..., None]
    lr, li = lam_re.astype(F32) * dt, lam_im.astype(F32) * dt

    def power(k):
        mag = jnp.exp(lr * k)
        return mag * jnp.cos(li * k), mag * jnp.sin(li * k)

    def cmul(ar, ai, br, bi):
        return ar * br - ai * bi, ar * bi + ai * br

    p1r, p1i = power(1.0)
    den = lam_re.astype(F32) ** 2 + lam_im.astype(F32) ** 2
    cfr, cfi = cmul(p1r - 1.0, p1i, lam_re.astype(F32) / den, -lam_im.astype(F32) / den)
    bbr, bbi = cmul(cfr[..., None], cfi[..., None], b_re.astype(F32)[None], b_im.astype(F32)[None])
    cr, ci = c_re.astype(F32), c_im.astype(F32)

    k = jnp.arange(T + 1, dtype=F32)[:, None, None, None]
    pwr, pwi = power(k)
    cbr, cbi = cmul(cr[..., None], ci[..., None], bbr[:, :, None], bbi[:, :, None])
    kern = (jnp.einsum("kdgp,dghpi->dkghi", pwr[:T], cbr, precision=HIGHEST)
            - jnp.einsum("kdgp,dghpi->dkghi", pwi[:T], cbi, precision=HIGHEST))
    two = jnp.concatenate([kern[1][:0:-1], (kern[0][0] + kern[1][0])[None], kern[0][1:],
                           jnp.zeros((1,) + kern.shape[2:], F32)], axis=0)
    k2 = two.transpose(1, 3, 2, 0).reshape(G, H * H, 2 * T)

    sf_r, sf_i = cmul(pwr[:T][::-1, 0, :, :, None], pwi[:T][::-1, 0, :, :, None], bbr[0][None], bbi[0][None])
    sb_r, sb_i = cmul(pwr[:T, 1, :, :, None], pwi[:T, 1, :, :, None], bbr[1][None], bbi[1][None])
    bmat = jnp.concatenate([sf_r, sb_r, sf_i, sb_i], axis=2)
    bmat = bmat.transpose(1, 3, 0, 2).reshape(G, H * T, 4 * P)

    mf_r, mf_i = cmul(cr[0][None], ci[0][None], pwr[1:, 0, :, None, :], pwi[1:, 0, :, None, :])
    mb_r, mb_i = cmul(cr[1][None], ci[1][None], pwr[1:][::-1, 1, :, None, :], pwi[1:][::-1, 1, :, None, :])
    zero = jnp.zeros_like(mf_r)
    cf = jnp.concatenate([mf_r, zero, -mf_i, zero], axis=3)
    cb = jnp.concatenate([zero, mb_r, zero, -mb_i], axis=3)
    cf = cf.transpose(1, 3, 2, 0).reshape(G, 4 * P, H * T)
    cb = cb.transpose(1, 3, 2, 0).reshape(G, 4 * P, H * T)

    lam_t = jnp.stack([jnp.concatenate([pwr[T, 0], pwr[T, 1]], axis=-1),
                       jnp.concatenate([pwi[T, 0], pwi[T, 1]], axis=-1)], axis=1)
    return k2, bmat.astype(BF16), cf.astype(BF16), cb.astype(BF16), lam_t


def _s5_kernel(*refs, ntrunk):
    u_refs = refs[:ntrunk]
    k2_ref, b_ref, cf_ref, cb_ref, lam_ref, d_ref = refs[ntrunk:ntrunk + 6]
    y_refs = refs[ntrunk + 6:2 * ntrunk + 6]
    tt, a_scr, sr_scr, si_scr, xf_scr, xb_scr = refs[2 * ntrunk + 6:]
    H, T = A_GROUP_CH, LANES

    def toeplitz(hp, carry):
        rows = k2_ref[0, pl.ds(pl.multiple_of(hp * H, H), H), :]
        for h in range(H):
            x = jnp.broadcast_to(rows[h:h + 1, :], (T, 2 * T))
            r = pltpu.roll(x, 1, axis=1, stride=1, stride_axis=0)
            tt[pl.ds(pl.multiple_of(hp * T, T), T), h * T:(h + 1) * T] = r[:, T:].astype(BF16)
        return carry

    lax.fori_loop(0, H, toeplitz, 0)

    lam_r = jnp.broadcast_to(lam_ref[0, 0:1, :], (SUBLANES, LANES))
    lam_i = jnp.broadcast_to(lam_ref[0, 1:2, :], (SUBLANES, LANES))
    fwd_lane = lax.broadcasted_iota(jnp.int32, (SUBLANES, LANES), 1) < A_STATE
    zero = jnp.zeros((SUBLANES, LANES), F32)

    for u_ref, y_ref in zip(u_refs, y_refs):
        nchunk = u_ref.shape[0]
        m = nchunk * SUBLANES
        for hp in range(H):
            a_scr[0:m, hp * T:(hp + 1) * T] = u_ref[:, hp].reshape(m, T).astype(BF16)
        s = _dot(a_scr[0:m, :], b_ref[0])
        sr_scr[0:m, :] = s[:, :LANES]
        si_scr[0:m, :] = s[:, LANES:]

        def step(i, carry, nchunk=nchunk):
            xr, xi = carry
            rf = pl.ds(pl.multiple_of(i * SUBLANES, SUBLANES), SUBLANES)
            rb = pl.ds(pl.multiple_of((nchunk - 1 - i) * SUBLANES, SUBLANES), SUBLANES)
            xf_scr[rf, 0:LANES] = xr
            xf_scr[rf, LANES:2 * LANES] = xi
            xb_scr[rb, 0:LANES] = xr
            xb_scr[rb, LANES:2 * LANES] = xi
            sr = jnp.where(fwd_lane, sr_scr[rf, :], sr_scr[rb, :])
            si = jnp.where(fwd_lane, si_scr[rf, :], si_scr[rb, :])
            return lam_r * xr - lam_i * xi + sr, lam_r * xi + lam_i * xr + si

        lax.fori_loop(0, nchunk, step, (zero, zero))

        a = a_scr[0:m, :]
        xf = xf_scr[0:m, :].astype(BF16)
        xb = xb_scr[0:m, :].astype(BF16)
        for j in range(H // 2):
            sl = slice(j * 2 * T, (j + 1) * 2 * T)
            yj = _dot(a, tt[:, sl]) + _dot(xf, cf_ref[0, :, sl]) + _dot(xb, cb_ref[0, :, sl])
            for k in range(2):
                h = 2 * j + k
                y_ref[:, h] = yj[:, k * T:(k + 1) * T].reshape(nchunk, SUBLANES, T) + u_ref[:, h] * d_ref[h]


def _s5(au4s, consts, d4):
    k2, bmat, cf, cb, lam_t = consts
    bsz = au4s[0].shape[2]
    assert bsz == SUBLANES
    G, H, T = A_GROUPS, A_GROUP_CH, LANES
    m = max(a.shape[0] for a in au4s) * bsz
    per_g = lambda g: (g, 0, 0)
    data = lambda a: pl.BlockSpec((a.shape[0], H, bsz, T), lambda g: (0, g, 0, 0))
    return pl.pallas_call(
        functools.partial(_s5_kernel, ntrunk=len(au4s)),
        grid=(G,),
        in_specs=[data(a) for a in au4s] + [
            pl.BlockSpec((1, H * H, 2 * T), per_g),
            pl.BlockSpec((1, H * T, 4 * A_STATE), per_g),
            pl.BlockSpec((1, 4 * A_STATE, H * T), per_g),
            pl.BlockSpec((1, 4 * A_STATE, H * T), per_g),
            pl.BlockSpec((1, 2, LANES), per_g),
            pl.BlockSpec((H, bsz, T), per_g)],
        out_specs=[data(a) for a in au4s],
        out_shape=[jax.ShapeDtypeStruct(a.shape, F32) for a in au4s],
        scratch_shapes=[pltpu.VMEM((H * T, H * T), BF16), pltpu.VMEM((m, H * T), BF16),
                        pltpu.VMEM((m, LANES), F32), pltpu.VMEM((m, LANES), F32),
                        pltpu.VMEM((m, 2 * LANES), F32), pltpu.VMEM((m, 2 * LANES), F32)],
        compiler_params=_params("arbitrary"),
        name="s5",
    )(*au4s, k2, bmat, cf, cb, lam_t, d4)


def _dft_tables(L):
    N = 2 * L
    n2 = LANES
    n1 = N // n2
    h = n1 // 2
    k = np.arange(n1)
    f1 = np.exp(-2j * np.pi * np.outer(k, k) / n1)
    f2 = np.exp(-2j * np.pi * np.outer(np.arange(n2), np.arange(n2)) / n2)
    tw = np.exp(-2j * np.pi * np.outer(k, np.arange(n2)) / N)
    f1m = np.block([[f1.real[:, :h], -f1.imag[:, :h]], [f1.imag[:, :h], f1.real[:, :h]]])
    ffm = np.concatenate([f1.real, f1.imag], axis=0)
    f2m = np.block([[f2.real, f2.imag], [-f2.imag, f2.real]])
    g2m = np.block([[f2.real, -f2.imag], [f2.imag, f2.real]])
    g1 = np.conj(f1)[:h, :]
    g1m = np.block([[g1.real, -g1.imag], [g1.imag, g1.real]])
    return dict(n1=n1, f1m=f1m, ffm=ffm, f2m=f2m, g2m=g2m, g1m=g1m, twr=tw.real, twi=tw.imag)


def _filter_positions(L):
    N = 2 * L
    i = np.arange(N)
    pos = np.where(i < L, i, N - i).astype(np.float32)
    t = pos / np.float32(max(L - 1, 1))
    w = np.float32(2.0 * math.pi) * pos / np.float32(L)
    bands = np.linspace(1e-4, HY_BANDS - 1, HY_BANDS, dtype=np.float32)
    wb = (w[:, None] * bands).astype(np.float32)
    feats = np.concatenate([t[:, None], np.cos(wb), -np.sin(wb)], axis=-1).astype(np.float32)
    fpad = np.zeros((LANES, N), np.float32)
    fpad[:HY_POS_FEAT] = feats.T
    keep = np.ones((N,), np.float32)
    keep[L] = 0.0
    n1 = N // LANES
    return fpad, t.reshape(n1, 1, LANES), keep.reshape(n1, 1, LANES)


def _filt_kernel(feat_ref, w1t_ref, b1_ref, w2t_ref, b2_ref, w3t_ref, sf_ref, ld_ref, t_ref, keep_ref,
                 ts_ref, scale_ref, ss_scr, dec_scr, *, inv_n, half_steps):
    i = pl.program_id(0)

    @pl.when(i % half_steps == 0)
    def _():
        dec_scr[...] = jnp.exp(ld_ref[0])

    @pl.when(i == 0)
    def _():
        ss_scr[...] = jnp.zeros_like(ss_scr)

    h1 = jnp.sin(sf_ref[:, 0:1] * (_dot(w1t_ref[...], feat_ref[...], precision=HIGHEST) + b1_ref[...]))
    h2 = jnp.sin(sf_ref[:, 1:2] * (_dot(w2t_ref[...], h1, precision=HIGHEST) + b2_ref[...]))
    w_hi, w_lo = _split_bf16(w3t_ref[0])
    h_hi, h_lo = _split_bf16(h2)
    ss = ss_scr[...]
    for k in range(FILT_SEG):
        sl = slice(k * LANES, (k + 1) * LANES)
        h3t = _dot(w_hi, h_hi[:, sl]) + _dot(w_hi, h_lo[:, sl]) + _dot(w_lo, h_hi[:, sl])
        win = jnp.exp(-dec_scr[...] * t_ref[k])
        ts = h3t * win * keep_ref[k]
        ts_ref[k] = ts.astype(BF16)
        ss = ss + ts * ts
    ss_scr[...] = ss

    @pl.when(i == pl.num_programs(0) - 1)
    def _():
        tot = jnp.sum(ss_scr[...], axis=-1, keepdims=True)
        scale_ref[...] = lax.rsqrt(tot + EPS) * inv_n


def _filt_fft_kernel(ts_ref, ffm_ref, twr_ref, twi_ref, f2m_ref, hr_ref, hi_ref, *, n1, cb):
    tr, ti = twr_ref[...], twi_ref[...]
    group = 8
    for g in range(cb // group):
        a = _dot(ffm_ref[...], ts_ref[:, g * group * LANES:(g + 1) * group * LANES])
        for c in range(group):
            sl = slice(c * LANES, (c + 1) * LANES)
            ar, ai = a[:n1, sl], a[n1:, sl]
            cat = jnp.concatenate([ar * tr - ai * ti, ar * ti + ai * tr], axis=1).astype(BF16)
            x = _dot(cat, f2m_ref[...]).astype(BF16)
            out = slice((g * group + c) * LANES, (g * group + c + 1) * LANES)
            hr_ref[:, out] = x[:, :LANES]
            hi_ref[:, out] = x[:, LANES:]


def _hyena_filters(L, w1, b1, w2, b2, w3, sin_freq, log_decay, cb=32):
    tabs = _dft_tables(L)
    n1 = tabs["n1"]
    N = 2 * L
    nch = HY_ORDER * B_WIDTH
    fpad, tpos, keep = _filter_positions(L)
    w1t = jnp.zeros((HY_FILT_HID, LANES), F32).at[:, :HY_POS_FEAT].set(w1.astype(F32).T)
    w3t = w3.astype(F32).T.reshape(2, nch, HY_FILT_HID)
    ld = log_decay.astype(F32).reshape(2, nch, 1)
    half = n1 // 2 // FILT_SEG
    col = lambda v: v.astype(F32).reshape(-1, 1)
    fixed = lambda shape: pl.BlockSpec(shape, lambda i: (0,) * len(shape))
    ts, scale = pl.pallas_call(
        functools.partial(_filt_kernel, inv_n=1.0 / N, half_steps=half),
        grid=(n1 // FILT_SEG,),
        in_specs=[pl.BlockSpec((LANES, FILT_SEG * LANES), lambda i: (0, i)),
                  fixed((HY_FILT_HID, LANES)), fixed((HY_FILT_HID, 1)),
                  fixed((HY_FILT_HID, HY_FILT_HID)), fixed((HY_FILT_HID, 1)),
                  pl.BlockSpec((1, nch, HY_FILT_HID), lambda i: (i // half, 0, 0)),
                  fixed((HY_FILT_HID, 2)),
                  pl.BlockSpec((1, nch, 1), lambda i: (i // half, 0, 0)),
                  pl.BlockSpec((FILT_SEG, 1, LANES), lambda i: (i, 0, 0)),
                  pl.BlockSpec((FILT_SEG, 1, LANES), lambda i: (i, 0, 0))],
        out_specs=[pl.BlockSpec((FILT_SEG, nch, LANES), lambda i: (i, 0, 0)),
                   pl.BlockSpec((nch, 1), lambda i: (0, 0))],
        out_shape=[jax.ShapeDtypeStruct((n1, nch, LANES), BF16),
                   jax.ShapeDtypeStruct((nch, 1), F32)],
        scratch_shapes=[pltpu.VMEM((nch, LANES), F32), pltpu.VMEM((nch, 1), F32)],
        compiler_params=_params("arbitrary"),
        name="hy_filter",
    )(jnp.asarray(fpad), w1t, col(b1), w2.astype(F32).T, col(b2), w3t, sin_freq.astype(F32).T, ld,
      jnp.asarray(tpos), jnp.asarray(keep))
    W = cb * LANES
    hr, hi = pl.pallas_call(
        functools.partial(_filt_fft_kernel, n1=n1, cb=cb),
        grid=(nch // cb,),
        in_specs=[pl.BlockSpec((n1, W), lambda j: (0, j)),
                  pl.BlockSpec((2 * n1, n1), lambda j: (0, 0)),
                  pl.BlockSpec((n1, LANES), lambda j: (0, 0)),
                  pl.BlockSpec((n1, LANES), lambda j: (0, 0)),
                  pl.BlockSpec((2 * LANES, 2 * LANES), lambda j: (0, 0))],
        out_specs=[pl.BlockSpec((n1, W), lambda j: (0, j)),
                   pl.BlockSpec((n1, W), lambda j: (0, j))],
        out_shape=[jax.ShapeDtypeStruct((n1, nch * LANES), BF16),
                   jax.ShapeDtypeStruct((n1, nch * LANES), BF16)],
        compiler_params=_params("arbitrary"),
        name="hy_filter_fft",
    )(ts.reshape(n1, nch * LANES), jnp.asarray(tabs["ffm"], F32).astype(BF16), jnp.asarray(tabs["twr"], F32),
      jnp.asarray(tabs["twi"], F32), jnp.asarray(tabs["f2m"], F32).astype(BF16))
    scale_rep = jnp.repeat(scale.reshape(HY_ORDER, B_WIDTH), LANES, axis=1)
    return hr, hi, scale_rep


def _shift_time(x, direction):
    rows = x.shape[0]
    lane = lax.broadcasted_iota(jnp.int32, x.shape, 1)
    row = lax.broadcasted_iota(jnp.int32, x.shape, 0)
    top = row[:SUBLANES]
    if direction > 0:
        r = pltpu.roll(x, 1, axis=1)
        e = pltpu.roll(r, 1, axis=0)
        edge = jnp.concatenate([jnp.where(top == 0, 0.0, e[:SUBLANES]), e[SUBLANES:]], axis=0)
        return jnp.where(lane == 0, edge, r)
    r = pltpu.roll(x, LANES - 1, axis=1)
    e = pltpu.roll(r, rows - 1, axis=0)
    edge = jnp.concatenate([e[:rows - SUBLANES], jnp.where(top == SUBLANES - 1, 0.0, e[rows - SUBLANES:])], axis=0)
    return jnp.where(lane == LANES - 1, edge, r)


def _hyena_kernel(g0_ref, g1_ref, z_ref, bg_ref, cw0_ref, cw1_ref, cwz_ref, skip_ref, scale_ref,
                  h0r_ref, h0i_ref, h1r_ref, h1i_ref, f1m_ref, g1m_ref, f2m_ref, g2m_ref, twr_ref, twi_ref,
                  o_ref, z_scr, g0_scr, g1_scr, a_scr, b_scr, *, n1, cb, bsz):
    nh = n1 // 2
    npair = bsz // 2
    tile = lambda i: slice(i * LANES, (i + 1) * LANES)

    def short_conv(src, cw_ref, dst):
        for c in range(cb):
            w0, w1, w2, bias = (cw_ref[k:k + 1, tile(c)] for k in range(4))
            for b in range(bsz):
                x = src[:, tile(c * bsz + b)].astype(F32)
                y = bias + w0 * _shift_time(x, 1) + w1 * x + w2 * _shift_time(x, -1)
                half, p = divmod(b, npair)
                dst[half * nh:(half + 1) * nh, tile(c * npair + p)] = y

    short_conv(g0_ref, cw0_ref, g0_scr)
    short_conv(g1_ref, cw1_ref, g1_scr)
    short_conv(z_ref, cwz_ref, z_scr)
    tr, ti = twr_ref[...].astype(BF16), twi_ref[...].astype(BF16)

    def long_conv(hr_ref, hi_ref):
        a_scr[...] = _dot(f1m_ref[...], z_scr[...].astype(BF16)).astype(BF16)
        for c in range(cb):
            hr, hi = hr_ref[:, tile(c)], hi_ref[:, tile(c)]
            for p in range(npair):
                sl = tile(c * npair + p)
                ar, ai = a_scr[0:n1, sl], a_scr[n1:2 * n1, sl]
                x = _dot(jnp.concatenate([ar * tr - ai * ti, ar * ti + ai * tr], axis=1), f2m_ref[...]).astype(BF16)
                xr, xi = x[:, :LANES], x[:, LANES:]
                y = _dot(jnp.concatenate([xr * hr - xi * hi, xr * hi + xi * hr], axis=1), g2m_ref[...]).astype(BF16)
                br, bi = y[:, :LANES], y[:, LANES:]
                b_scr[0:n1, sl] = br * tr + bi * ti
                b_scr[n1:2 * n1, sl] = bi * tr - br * ti
        return _dot(g1m_ref[...], b_scr[...])

    def per_channel(ref, row):
        return jnp.concatenate([ref[row:row + 1, tile(c)] for c in range(cb) for _ in range(npair)], axis=1)

    z0 = z_scr[...]
    z1 = g0_scr[...] * (long_conv(h0r_ref, h0i_ref) * per_channel(scale_ref, 0) + z0 * per_channel(skip_ref, 0))
    z_scr[...] = z1
    z2 = g1_scr[...] * (long_conv(h1r_ref, h1i_ref) * per_channel(scale_ref, 1) + z1 * per_channel(skip_ref, 1))
    z_scr[...] = z2
    for c in range(cb):
        for b in range(bsz):
            half, p = divmod(b, npair)
            gate = _silu(bg_ref[:, tile(c * bsz + b)].astype(F32))
            o_ref[:, tile(c * bsz + b)] = (z_scr[half * nh:(half + 1) * nh, tile(c * npair + p)] * gate).astype(BF16)


def _hyena(pt2, filt, conv_w, conv_b, skip, L, bsz, cb=8):
    hr, hi, scale_rep = filt
    tabs = _dft_tables(L)
    n1 = tabs["n1"]
    nh = n1 // 2
    assert nh % SUBLANES == 0, "sequence length must be a multiple of 1024"
    wd = cb * bsz * LANES
    wc = cb * LANES
    ws = cb * (bsz // 2) * LANES
    ncb = B_WIDTH // cb
    cw = jnp.repeat(jnp.concatenate([conv_w.astype(F32), conv_b.astype(F32)[None]], axis=0), LANES, axis=1)
    sk = jnp.repeat(skip.astype(F32), LANES, axis=1)
    data = lambda row0: pl.BlockSpec((nh, wd), lambda j: (0, row0 // cb + j))
    cws = lambda grp: pl.BlockSpec((4, wc), lambda j: (0, grp * ncb + j))
    hs = lambda order: pl.BlockSpec((n1, wc), lambda j: (0, order * ncb + j))
    const = lambda j: (0, 0)
    return pl.pallas_call(
        functools.partial(_hyena_kernel, n1=n1, cb=cb, bsz=bsz),
        grid=(ncb,),
        in_specs=[data(CT_BU), data(CT_BU + B_WIDTH), data(CT_BU + 2 * B_WIDTH), data(CT_BG),
                  cws(0), cws(1), cws(2),
                  pl.BlockSpec((2, wc), lambda j: (0, j)), pl.BlockSpec((2, wc), lambda j: (0, j)),
                  hs(0), hs(0), hs(1), hs(1),
                  pl.BlockSpec((2 * n1, n1), const), pl.BlockSpec((n1, 2 * n1), const),
                  pl.BlockSpec((2 * LANES, 2 * LANES), const), pl.BlockSpec((2 * LANES, 2 * LANES), const),
                  pl.BlockSpec((n1, LANES), const), pl.BlockSpec((n1, LANES), const)],
        out_specs=pl.BlockSpec((nh, wd), lambda j: (0, j)),
        out_shape=jax.ShapeDtypeStruct((nh, B_WIDTH * bsz * LANES), BF16),
        scratch_shapes=[pltpu.VMEM((n1, ws), F32), pltpu.VMEM((n1, ws), F32), pltpu.VMEM((n1, ws), F32),
                        pltpu.VMEM((2 * n1, ws), BF16), pltpu.VMEM((2 * n1, ws), BF16)],
        compiler_params=_params("arbitrary"),
        name="hyena",
    )(pt2, pt2, pt2, pt2, cw, cw, cw, sk, scale_rep, hr, hi, hr, hi,
      jnp.asarray(tabs["f1m"], F32).astype(BF16), jnp.asarray(tabs["g1m"], F32).astype(BF16),
      jnp.asarray(tabs["f2m"], F32).astype(BF16), jnp.asarray(tabs["g2m"], F32).astype(BF16),
      jnp.asarray(tabs["twr"], F32), jnp.asarray(tabs["twi"], F32))


def _gla_prep(q, z, v, lb, tri, forward):
    C = GLA_CHUNK
    half, quarter = C // 2, C // 4
    one_m = 1.0 - lb
    sig = _sigmoid(z)
    f = lb + one_m * sig
    logf = jnp.log2(jnp.maximum(f, F_FLOOR))
    kk = one_m * (1.0 - sig)
    l_hi, l_lo = _split_bf16(logf)
    acc = _dot(tri, jnp.concatenate([l_hi, l_lo], axis=0))
    if forward:
        tot = acc[C - 1:C]
        a_low, a_high, a_cross = acc[quarter - 1:quarter], acc[half + quarter - 1:half + quarter], acc[half - 1:half]
    else:
        tot = acc[0:1]
        a_low, a_high, a_cross = acc[quarter:quarter + 1], acc[half + quarter:half + quarter + 1], acc[half:half + 1]
    e_leaf = jnp.concatenate([acc[:half] - a_low, acc[half:] - a_high], axis=0)
    ql = (q * jnp.exp2(e_leaf)).astype(BF16)
    kl = (kk * jnp.exp2(-e_leaf)).astype(BF16)
    if forward:
        qc = (q[half:] * jnp.exp2(acc[half:] - a_cross)).astype(BF16)
        kc = (kk[:half] * jnp.exp2(a_cross - acc[:half])).astype(BF16)
    else:
        qc = (q[:half] * jnp.exp2(acc[:half] - a_cross)).astype(BF16)
        kc = (kk[half:] * jnp.exp2(a_cross - acc[half:])).astype(BF16)
    qi = (q * jnp.exp2(acc)).astype(BF16)
    ks = (kk * jnp.exp2(tot - acc)).astype(BF16)
    return ql, kl, qc, kc, qi, ks, jnp.exp2(tot), v.astype(BF16)


def _gla_matmuls(ops, st_ref, o_ref, rows, forward, p):
    ql, kl, qc, kc, qi, ks, dec, vb = ops
    C = GLA_CHUNK
    half = C // 2
    r2 = lax.broadcasted_iota(jnp.int32, (LANES, LANES), 0) < C_HEAD_DIM
    c2 = lax.broadcasted_iota(jnp.int32, (LANES, LANES), 1) < C_HEAD_DIM
    same_head = r2 == c2

    def by_head(x):
        first = lax.broadcasted_iota(jnp.int32, x.shape, 1) % LANES < C_HEAD_DIM
        return jnp.concatenate([x * jnp.where(first, 1.0, 0.0).astype(BF16),
                                x * jnp.where(first, 0.0, 1.0).astype(BF16)], axis=0)

    zeros = jnp.zeros((half, LANES), BF16)
    ti = lax.broadcasted_iota(jnp.int32, (2 * half, C), 0) % half
    si = lax.broadcasted_iota(jnp.int32, (2 * half, C), 1)
    if forward:
        own_lo = _dot_nt(by_head(ql[:half]), jnp.concatenate([kl[:half], zeros], axis=0))
        mix_hi = _dot_nt(by_head(jnp.concatenate([ql[half:], qc], axis=1)),
                         jnp.concatenate([jnp.concatenate([zeros, kl[half:]], axis=0),
                                          jnp.concatenate([kc, zeros], axis=0)], axis=1))
        s_lo = jnp.where(ti >= si, own_lo, 0.0)
        s_hi = jnp.where(ti + half >= si, mix_hi, 0.0)
    else:
        own_hi = _dot_nt(by_head(ql[half:]), jnp.concatenate([zeros, kl[half:]], axis=0))
        mix_lo = _dot_nt(by_head(jnp.concatenate([ql[:half], qc], axis=1)),
                         jnp.concatenate([jnp.concatenate([kl[:half], zeros], axis=0),
                                          jnp.concatenate([zeros, kc], axis=0)], axis=1))
        s_lo = jnp.where(si >= ti, mix_lo, 0.0)
        s_hi = jnp.where(si >= ti + half, own_hi, 0.0)
    s = jnp.concatenate([jnp.concatenate([s_lo[:half], s_hi[:half]], axis=0),
                         jnp.concatenate([s_lo[half:], s_hi[half:]], axis=0)], axis=1).astype(BF16)
    intra = _dot(s, by_head(vb))
    st = st_ref[p]
    inter = _dot_nt(qi, st.astype(BF16))
    o_ref[0, rows, p * LANES:(p + 1) * LANES] = (intra + inter).astype(o_ref.dtype)
    upd = _dot_tn(vb, ks)
    st_ref[p] = st * dec + jnp.where(same_head, upd, 0.0)


def _hgrn_kernel(qf_ref, zf_ref, vf_ref, qb_ref, zb_ref, vb_ref, lb_ref, tri_ref, of_ref, ob_ref, stf, stb, *, nsub):
    @pl.when(pl.program_id(1) == 0)
    def _():
        stf[...] = jnp.zeros_like(stf)
        stb[...] = jnp.zeros_like(stb)

    items = []
    for j in range(nsub):
        for p in range(C_WIDTH // LANES):
            items.append((qf_ref, zf_ref, vf_ref, of_ref, stf, 0, slice(j * GLA_CHUNK, (j + 1) * GLA_CHUNK), True, p))
            items.append((qb_ref, zb_ref, vb_ref, ob_ref, stb, 1,
                          slice((nsub - 1 - j) * GLA_CHUNK, (nsub - j) * GLA_CHUNK), False, p))

    def prep(item):
        q_ref, z_ref, v_ref, _, _, d, rows, forward, p = item
        sl = slice(p * LANES, (p + 1) * LANES)
        return _gla_prep(q_ref[0, rows, sl].astype(F32), z_ref[0, rows, sl].astype(F32),
                         v_ref[0, rows, sl].astype(F32), lb_ref[d:d + 1, sl], tri_ref[d], forward)

    ops = prep(items[0])
    for i, item in enumerate(items):
        nxt = prep(items[i + 1]) if i + 1 < len(items) else None
        _gla_matmuls(ops, item[4], item[3], item[6], item[7], item[8])
        ops = nxt


def _hgrn(c_all, lb, rb):
    bsz, L, _ = c_all.shape
    nblk = L // rb
    w = C_WIDTH
    fwd = lambda col: pl.BlockSpec((1, rb, w), lambda b, i: (b, i, col))
    bwd = lambda col: pl.BlockSpec((1, rb, w), lambda b, i: (b, nblk - 1 - i, col))
    npair = w // LANES
    idx = np.arange(GLA_CHUNK)
    tri = np.stack([idx[:, None] >= idx[None, :], idx[None, :] >= idx[:, None]])
    tri = jnp.asarray(np.concatenate([tri, tri], axis=2), BF16)
    return pl.pallas_call(
        functools.partial(_hgrn_kernel, nsub=rb // GLA_CHUNK),
        grid=(bsz, nblk),
        in_specs=[fwd(0), fwd(1), fwd(3), bwd(0), bwd(2), bwd(3), pl.BlockSpec((2, w), lambda b, i: (0, 0)),
                  pl.BlockSpec((2, GLA_CHUNK, 2 * GLA_CHUNK), lambda b, i: (0, 0, 0))],
        out_specs=[pl.BlockSpec((1, rb, w), lambda b, i: (b, i, 0)),
                   pl.BlockSpec((1, rb, w), lambda b, i: (b, nblk - 1 - i, 0))],
        out_shape=[jax.ShapeDtypeStruct((bsz, L, w), BF16), jax.ShapeDtypeStruct((bsz, L, w), BF16)],
        scratch_shapes=[pltpu.VMEM((npair, LANES, LANES), F32), pltpu.VMEM((npair, LANES, LANES), F32)],
        compiler_params=_params("arbitrary", "arbitrary"),
        name="hgrn",
    )(c_all, c_all, c_all, c_all, c_all, c_all, lb, tri)


def _gelu_tanh(x):
    return 0.5 * x * (1.0 + jnp.tanh(math.sqrt(2.0 / math.pi) * (x + 0.044715 * (x * x * x))))


def _out_kernel(x_ref, mod_ref, ys_ref, ag_ref, hy_ref, of_ref, ob_ref, cg_ref, gw_ref, gb_ref,
                hn_ref, ones_ref, wo_ref, fn_ref, o_ref, *, final):
    bsz = x_ref.shape[0]
    m = bsz * LANES
    y_rows, g_rows, b_rows = [], [], []
    for b in range(bsz):
        lanes = slice(b * LANES, (b + 1) * LANES)
        y_rows.append(ys_ref[0, :, b, :].T)
        g_rows.append(ag_ref[0, :, lanes].astype(F32).T)
        b_rows.append(hy_ref[0, :, lanes].astype(F32).T)
    z = _gelu_tanh(jnp.concatenate(y_rows, axis=0))
    glu = _sigmoid(_dot(z.astype(BF16), gw_ref[...]) + gb_ref[...])
    a_out = (z * glu * _silu(jnp.concatenate(g_rows, axis=0))).astype(BF16)
    b_out = jnp.concatenate(b_rows, axis=0).astype(BF16)

    o = (of_ref[...].astype(F32) + ob_ref[...].astype(F32)).reshape(m, C_WIDTH)
    ms = _dot((o * o).astype(BF16), ones_ref[...]) * (1.0 / C_HEAD_DIM)
    gate_c = _silu(cg_ref[...].astype(F32)).reshape(m, C_WIDTH)
    c_out = (o * lax.rsqrt(ms + EPS) * hn_ref[...] * gate_c).astype(BF16)

    mixed = _dot(jnp.concatenate([a_out, b_out, c_out], axis=1), wo_ref[...])
    xn = x_ref[...] + mod_ref[:, 2:3, :] * mixed.reshape(bsz, LANES, D_MODEL)
    if final:
        msq = jnp.mean(xn * xn, axis=-1, keepdims=True)
        xn = xn * lax.rsqrt(msq + EPS) * fn_ref[...]
    o_ref[...] = xn


def _out(x, mod, ys4, pt3, hy3, of, ob, c_all, gw, gb, hn, ones_bd, wo, fn, final):
    bsz, L, dm = x.shape
    row = lambda w, col: pl.BlockSpec((bsz, LANES, w), lambda i: (0, i, col))
    full = lambda a: _resident(a.shape, lambda i: (0,) * a.ndim)
    return pl.pallas_call(
        functools.partial(_out_kernel, final=final),
        grid=(L // LANES,),
        in_specs=[row(dm, 0), full(mod),
                  pl.BlockSpec((1, A_WIDTH, bsz, LANES), lambda i: (i, 0, 0, 0)),
                  pl.BlockSpec((1, A_WIDTH, bsz * LANES), lambda i: (i, CT_AG // A_WIDTH, 0)),
                  pl.BlockSpec((1, B_WIDTH, bsz * LANES), lambda i: (i, 0, 0)),
                  row(C_WIDTH, 0), row(C_WIDTH, 0), row(C_WIDTH, 4),
                  full(gw), full(gb), full(hn), full(ones_bd), full(wo), full(fn)],
        out_specs=row(dm, 0),
        out_shape=jax.ShapeDtypeStruct((bsz, L, dm), F32),
        compiler_params=_params("arbitrary"),
        name="out",
    )(x, mod, ys4, pt3, hy3, of, ob, c_all, gw, gb, hn, ones_bd, wo, fn)


def _layer(xs, mods, p, layer, hg_rb=2048):
    head = np.arange(C_WIDTH) // C_HEAD_DIM
    ones_bd = jnp.asarray(head[:, None] == head[None, :], BF16)
    w_in = p["w_in"][layer]
    o_b = 2 * A_WIDTH + 4 * B_WIDTH
    wt = w_in[:, :o_b].T.astype(BF16)
    wc = w_in[:, o_b:].astype(BF16)
    norm_w = p["norm_w"][layer].reshape(1, -1).astype(F32)
    projs = [_proj(x, mod, norm_w, wt, wc) for x, mod in zip(xs, mods)]
    bsz = xs[0].shape[0]
    d4 = jnp.broadcast_to(p["s5_d"][layer].astype(F32)[:, None, None], (A_WIDTH, bsz, LANES))
    ys4s = _s5([au for au, _, _ in projs], p["s5_consts"][layer], d4)
    out = []
    for x, mod, (au, pt, c_all), ys4 in zip(xs, mods, projs, ys4s):
        L = x.shape[1]
        nchunk = L // LANES
        hy2 = _hyena(pt.reshape(nchunk, CT_ROWS * bsz * LANES), p["hy_filters"][(layer, L)], p["hy_conv_w"][layer],
                     p["hy_conv_b"][layer], p["hy_skip"][layer], L, bsz)
        of, ob = _hgrn(c_all, p["lower_bounds"][layer], hg_rb)
        out.append(_out(x, mod, ys4, pt, hy2.reshape(nchunk, B_WIDTH, bsz * LANES), of, ob, c_all,
                        p["s5_glu_w"][layer].astype(BF16), p["s5_glu_b"][layer].reshape(1, -1).astype(F32),
                        jnp.tile(p["hg_norm_w"][layer].astype(F32), C_WIDTH // C_HEAD_DIM).reshape(1, -1),
                        ones_bd, p["w_out"][layer].astype(BF16), p["final_norm_w"].reshape(1, -1).astype(F32),
                        layer == DEPTH - 1))
    return out


def kernel(x_prompt, x_sample, c_prompt, c_sample, norm_w, ada_w, ada_b, w_in, w_out, s5_lambda_re, s5_lambda_im, s5_log_dt, s5_b_re, s5_b_im, s5_c_re, s5_c_im, s5_d, s5_glu_w, s5_glu_b, hy_conv_w, hy_conv_b, hy_w1, hy_b1, hy_w2, hy_b2, hy_w3, hy_sin_freq, hy_log_decay, hy_skip, hg_lb_logits, hg_norm_w, final_norm_w):
    bp = x_prompt.shape[0]
    ada = _ada(jnp.concatenate([c_prompt, c_sample], axis=0).astype(F32), ada_w.astype(F32), ada_b.astype(F32))
    lb_soft = jax.nn.softmax(hg_lb_logits.astype(F32), axis=0)
    lower_bounds = jnp.cumsum(lb_soft, axis=0) - lb_soft[0:1]
    lengths = sorted({x_prompt.shape[1], x_sample.shape[1]})
    p = dict(
        norm_w=norm_w, w_in=w_in, w_out=w_out, s5_d=s5_d, s5_glu_w=s5_glu_w, s5_glu_b=s5_glu_b,
        hy_conv_w=hy_conv_w, hy_conv_b=hy_conv_b, hy_skip=hy_skip, hg_norm_w=hg_norm_w, final_norm_w=final_norm_w,
        lower_bounds=lower_bounds,
        s5_consts=[_s5_consts(s5_lambda_re[l], s5_lambda_im[l], s5_log_dt[l], s5_b_re[l], s5_b_im[l],
                              s5_c_re[l], s5_c_im[l]) for l in range(DEPTH)],
        hy_filters={(l, L): _hyena_filters(L, hy_w1[l], hy_b1[l], hy_w2[l], hy_b2[l], hy_w3[l], hy_sin_freq[l],
                                           hy_log_decay[l])
                    for l in range(DEPTH) for L in lengths},
    )
    xs = [x_prompt, x_sample]
    for layer in range(DEPTH):
        mods = [ada[layer, :bp].reshape(bp, 3, D_MODEL),
                ada[layer, bp:].reshape(x_sample.shape[0], 3, D_MODEL)]
        xs = _layer(xs, mods, p, layer)
    return (xs[0], xs[1])
```

```python
import functools
import math

import numpy as np
import jax
import jax.numpy as jnp
from jax import lax
from jax.experimental import pallas as pl
from jax.experimental.pallas import tpu as pltpu

F32 = jnp.float32
BF16 = jnp.bfloat16
HIGHEST = lax.Precision.HIGHEST

D_MODEL = 1024
DEPTH = 2
A_WIDTH = 256
A_GROUP_CH = 16
A_GROUPS = 16
A_STATE = 64
B_WIDTH = 384
HY_ORDER = 2
HY_BANDS = 16
HY_POS_FEAT = 1 + 2 * HY_BANDS
HY_FILT_HID = 64
C_WIDTH = 384
C_HEAD_DIM = 64
EPS = 1e-6
F_FLOOR = 1e-30

LANES = 128
SUBLANES = 8
GLA_CHUNK = 128
FILT_SEG = 8
VMEM_LIMIT = 56 * 1024 * 1024

CT_AG, CT_BU, CT_BG = 0, A_WIDTH, A_WIDTH + 3 * B_WIDTH
CT_ROWS = A_WIDTH + 4 * B_WIDTH
NAT_COLS = 5 * C_WIDTH
NAT_PIECE = 768


def _dot(a, b, precision=None):
    return jnp.dot(a, b, preferred_element_type=F32, precision=precision)


def _dot_nt(a, b, precision=None):
    return lax.dot_general(a, b, (((1,), (1,)), ((), ())), preferred_element_type=F32, precision=precision)


def _dot_tn(a, b, precision=None):
    return lax.dot_general(a, b, (((0,), (0,)), ((), ())), preferred_element_type=F32, precision=precision)


def _split_bf16(a):
    hi = a.astype(BF16)
    return hi, (a - hi.astype(F32)).astype(BF16)


def _sigmoid(x):
    return 1.0 / (1.0 + jnp.exp(-x))


def _silu(x):
    return x * _sigmoid(x)


def _params(*sem):
    return pltpu.CompilerParams(dimension_semantics=sem, vmem_limit_bytes=VMEM_LIMIT)


def _resident(shape, index_map):
    return pl.BlockSpec(shape, index_map, pipeline_mode=pl.Buffered(1))


def _ada_kernel(c_ref, w_ref, b_ref, o_ref):
    cond = _silu(c_ref[...])
    o_ref[0] = _dot(cond, w_ref[0], precision=HIGHEST) + b_ref[0]


def _ada(c_all, ada_w, ada_b):
    nb = c_all.shape[0]
    d = D_MODEL
    return pl.pallas_call(
        _ada_kernel,
        grid=(DEPTH, 3),
        in_specs=[pl.BlockSpec((nb, d), lambda l, j: (0, 0)),
                  pl.BlockSpec((1, d, d), lambda l, j: (l, 0, j)),
                  pl.BlockSpec((1, 1, d), lambda l, j: (l, 0, j))],
        out_specs=pl.BlockSpec((1, nb, d), lambda l, j: (l, 0, j)),
        out_shape=jax.ShapeDtypeStruct((DEPTH, nb, 3 * d), F32),
        compiler_params=_params("arbitrary", "arbitrary"),
        name="ada",
    )(c_all, ada_w, ada_b.reshape(DEPTH, 1, 3 * d))


def _proj_kernel(x_ref, mod_ref, nw_ref, wt_ref, wc_ref, au_ref, pt_ref, c_ref):
    bsz = x_ref.shape[0]
    nb = 2
    rows = 256
    for s in range(bsz // nb):
        bs = slice(s * nb, (s + 1) * nb)
        x = x_ref[bs]
        ms = jnp.mean(x * x, axis=-1, keepdims=True)
        h = x * lax.rsqrt(ms + EPS) * nw_ref[...]
        h = h * (1.0 + mod_ref[bs, 1:2, :]) + mod_ref[bs, 0:1, :]
        hb = h.astype(BF16).reshape(nb * LANES, D_MODEL)
        ls = slice(s * nb * LANES, (s + 1) * nb * LANES)
        au = _dot_nt(wt_ref[0:A_WIDTH, :], hb)
        for k in range(nb):
            au_ref[0, :, s * nb + k, :] = au[:, k * LANES:(k + 1) * LANES]
        for r in range(0, CT_ROWS, rows):
            pt_ref[0, r:r + rows, ls] = _dot_nt(wt_ref[A_WIDTH + r:A_WIDTH + r + rows, :], hb).astype(BF16)
        for c0 in range(0, NAT_COLS, NAT_PIECE):
            sl = slice(c0, min(c0 + NAT_PIECE, NAT_COLS))
            c_ref[bs, :, sl] = _dot(hb, wc_ref[:, sl]).astype(BF16).reshape(nb, LANES, sl.stop - sl.start)


def _proj(x, mod, norm_w, wt, wc):
    bsz, L, d = x.shape
    nchunk = L // LANES
    return pl.pallas_call(
        _proj_kernel,
        grid=(nchunk,),
        in_specs=[pl.BlockSpec((bsz, LANES, d), lambda i: (0, i, 0)),
                  _resident((bsz, 3, d), lambda i: (0, 0, 0)),
                  _resident((1, d), lambda i: (0, 0)),
                  _resident(wt.shape, lambda i: (0, 0)),
                  _resident(wc.shape, lambda i: (0, 0))],
        out_specs=[pl.BlockSpec((1, A_WIDTH, bsz, LANES), lambda i: (i, 0, 0, 0)),
                   pl.BlockSpec((1, CT_ROWS, bsz * LANES), lambda i: (i, 0, 0)),
                   pl.BlockSpec((bsz, LANES, NAT_COLS), lambda i: (0, i, 0))],
        out_shape=[jax.ShapeDtypeStruct((nchunk, A_WIDTH, bsz, LANES), F32),
                   jax.ShapeDtypeStruct((nchunk, CT_ROWS, bsz * LANES), BF16),
                   jax.ShapeDtypeStruct((bsz, L, NAT_COLS), BF16)],
        compiler_params=_params("arbitrary"),
        name="proj",
    )(x, mod, norm_w, wt, wc)


def _s5_consts(lam_re, lam_im, log_dt, b_re, b_im, c_re, c_im):
    T = LANES
    G, P, H = A_GROUPS, A_STATE, A_GROUP_CH
    dt = jnp.exp(log_dt.astype(F32))[..., None]
    lr, li = lam_re.astype(F32) * dt, lam_im.astype(F32) * dt

    def power(k):
        mag = jnp.exp(lr * k)
        return mag * jnp.cos(li * k), mag * jnp.sin(li * k)

    def cmul(ar, ai, br, bi):
        return ar * br - ai * bi, ar * bi + ai * br

    p1r, p1i = power(1.0)
    den = lam_re.astype(F32) ** 2 + lam_im.astype(F32) ** 2
    cfr, cfi = cmul(p1r - 1.0, p1i, lam_re.astype(F32) / den, -lam_im.astype(F32) / den)
    bbr, bbi = cmul(cfr[..., None], cfi[..., None], b_re.astype(F32)[None], b_im.astype(F32)[None])
    cr, ci = c_re.astype(F32), c_im.astype(F32)

    k = jnp.arange(T + 1, dtype=F32)[:, None, None, None]
    pwr, pwi = power(k)
    cbr, cbi = cmul(cr[..., None], ci[..., None], bbr[:, :, None], bbi[:, :, None])
    kern = (jnp.einsum("kdgp,dghpi->dkghi", pwr[:T], cbr, precision=HIGHEST)
            - jnp.einsum("kdgp,dghpi->dkghi", pwi[:T], cbi, precision=HIGHEST))
    two = jnp.concatenate([kern[1][:0:-1], (kern[0][0] + kern[1][0])[None], kern[0][1:],
                           jnp.zeros((1,) + kern.shape[2:], F32)], axis=0)
    k2 = two.transpose(1, 3, 2, 0).reshape(G, H * H, 2 * T)

    sf_r, sf_i = cmul(pwr[:T][::-1, 0, :, :, None], pwi[:T][::-1, 0, :, :, None], bbr[0][None], bbi[0][None])
    sb_r, sb_i = cmul(pwr[:T, 1, :, :, None], pwi[:T, 1, :, :, None], bbr[1][None], bbi[1][None])
    bmat = jnp.concatenate([sf_r, sb_r, sf_i, sb_i], axis=2)
    bmat = bmat.transpose(1, 3, 0, 2).reshape(G, H * T, 4 * P)

    mf_r, mf_i = cmul(cr[0][None], ci[0][None], pwr[1:, 0, :, None, :], pwi[1:, 0, :, None, :])
    mb_r, mb_i = cmul(cr[1][None], ci[1][None], pwr[1:][::-1, 1, :, None, :], pwi[1:][::-1, 1, :, None, :])
    zero = jnp.zeros_like(mf_r)
    cf = jnp.concatenate([mf_r, zero, -mf_i, zero], axis=3)
    cb = jnp.concatenate([zero, mb_r, zero, -mb_i], axis=3)
    cf = cf.transpose(1, 3, 2, 0).reshape(G, 4 * P, H * T)
    cb = cb.transpose(1, 3, 2, 0).reshape(G, 4 * P, H * T)

    lam_t = jnp.stack([jnp.concatenate([pwr[T, 0], pwr[T, 1]], axis=-1),
                       jnp.concatenate([pwi[T, 0], pwi[T, 1]], axis=-1)], axis=1)
    return k2, bmat.astype(BF16), cf.astype(BF16), cb.astype(BF16), lam_t


def _s5_kernel(*refs, ntrunk):
    u_refs = refs[:ntrunk]
    k2_ref, b_ref, cf_ref, cb_ref, lam_ref, d_ref = refs[ntrunk:ntrunk + 6]
    y_refs = refs[ntrunk + 6:2 * ntrunk + 6]
    tt, a_scr, sr_scr, si_scr, xf_scr, xb_scr = refs[2 * ntrunk + 6:]
    H, T = A_GROUP_CH, LANES

    def toeplitz(hp, carry):
        rows = k2_ref[0, pl.ds(pl.multiple_of(hp * H, H), H), :]
        for h in range(H):
            x = jnp.broadcast_to(rows[h:h + 1, :], (T, 2 * T))
            r = pltpu.roll(x, 1, axis=1, stride=1, stride_axis=0)
            tt[pl.ds(pl.multiple_of(hp * T, T), T), h * T:(h + 1) * T] = r[:, T:].astype(BF16)
        return carry

    lax.fori_loop(0, H, toeplitz, 0)

    lam_r = jnp.broadcast_to(lam_ref[0, 0:1, :], (SUBLANES, LANES))
    lam_i = jnp.broadcast_to(lam_ref[0, 1:2, :], (SUBLANES, LANES))
    fwd_lane = lax.broadcasted_iota(jnp.int32, (SUBLANES, LANES), 1) < A_STATE
    zero = jnp.zeros((SUBLANES, LANES), F32)

    for u_ref, y_ref in zip(u_refs, y_refs):
        nchunk = u_ref.shape[0]
        m = nchunk * SUBLANES
        for hp in range(H):
            a_scr[0:m, hp * T:(hp + 1) * T] = u_ref[:, hp].reshape(m, T).astype(BF16)
        s = _dot(a_scr[0:m, :], b_ref[0])
        sr_scr[0:m, :] = s[:, :LANES]
        si_scr[0:m, :] = s[:, LANES:]

        def step(i, carry, nchunk=nchunk):
            xr, xi = carry
            rf = pl.ds(pl.multiple_of(i * SUBLANES, SUBLANES), SUBLANES)
            rb = pl.ds(pl.multiple_of((nchunk - 1 - i) * SUBLANES, SUBLANES), SUBLANES)
            xf_scr[rf, 0:LANES] = xr
            xf_scr[rf, LANES:2 * LANES] = xi
            xb_scr[rb, 0:LANES] = xr
            xb_scr[rb, LANES:2 * LANES] = xi
            sr = jnp.where(fwd_lane, sr_scr[rf, :], sr_scr[rb, :])
            si = jnp.where(fwd_lane, si_scr[rf, :], si_scr[rb, :])
            return lam_r * xr - lam_i * xi + sr, lam_r * xi + lam_i * xr + si

        lax.fori_loop(0, nchunk, step, (zero, zero))

        a = a_scr[0:m, :]
        xf = xf_scr[0:m, :].astype(BF16)
        xb = xb_scr[0:m, :].astype(BF16)
        for j in range(H // 2):
            sl = slice(j * 2 * T, (j + 1) * 2 * T)
            yj = _dot(a, tt[:, sl]) + _dot(xf, cf_ref[0, :, sl]) + _dot(xb, cb_ref[0, :, sl])
            for k in range(2):
                h = 2 * j + k
                y_ref[:, h] = yj[:, k * T:(k + 1) * T].reshape(nchunk, SUBLANES, T) + u_ref[:, h] * d_ref[h]


def _s5(au4s, consts, d4):
    k2, bmat, cf, cb, lam_t = consts
    bsz = au4s[0].shape[2]
    assert bsz == SUBLANES
    G, H, T = A_GROUPS, A_GROUP_CH, LANES
    m = max(a.shape[0] for a in au4s) * bsz
    per_g = lambda g: (g, 0, 0)
    data = lambda a: pl.BlockSpec((a.shape[0], H, bsz, T), lambda g: (0, g, 0, 0))
    return pl.pallas_call(
        functools.partial(_s5_kernel, ntrunk=len(au4s)),
        grid=(G,),
        in_specs=[data(a) for a in au4s] + [
            pl.BlockSpec((1, H * H, 2 * T), per_g),
            pl.BlockSpec((1, H * T, 4 * A_STATE), per_g),
            pl.BlockSpec((1, 4 * A_STATE, H * T), per_g),
            pl.BlockSpec((1, 4 * A_STATE, H * T), per_g),
            pl.BlockSpec((1, 2, LANES), per_g),
            pl.BlockSpec((H, bsz, T), per_g)],
        out_specs=[data(a) for a in au4s],
        out_shape=[jax.ShapeDtypeStruct(a.shape, F32) for a in au4s],
        scratch_shapes=[pltpu.VMEM((H * T, H * T), BF16), pltpu.VMEM((m, H * T), BF16),
                        pltpu.VMEM((m, LANES), F32), pltpu.VMEM((m, LANES), F32),
                        pltpu.VMEM((m, 2 * LANES), F32), pltpu.VMEM((m, 2 * LANES), F32)],
        compiler_params=_params("arbitrary"),
        name="s5",
    )(*au4s, k2, bmat, cf, cb, lam_t, d4)


def _dft_tables(L):
    N = 2 * L
    n2 = LANES
    n1 = N // n2
    h = n1 // 2
    k = np.arange(n1)
    f1 = np.exp(-2j * np.pi * np.outer(k, k) / n1)
    f2 = np.exp(-2j * np.pi * np.outer(np.arange(n2), np.arange(n2)) / n2)
    tw = np.exp(-2j * np.pi * np.outer(k, np.arange(n2)) / N)
    f1m = np.block([[f1.real[:, :h], -f1.imag[:, :h]], [f1.imag[:, :h], f1.real[:, :h]]])
    ffm = np.concatenate([f1.real, f1.imag], axis=0)
    f2m = np.block([[f2.real, f2.imag], [-f2.imag, f2.real]])
    g2m = np.block([[f2.real, -f2.imag], [f2.imag, f2.real]])
    g1 = np.conj(f1)[:h, :]
    g1m = np.block([[g1.real, -g1.imag], [g1.imag, g1.real]])
    return dict(n1=n1, f1m=f1m, ffm=ffm, f2m=f2m, g2m=g2m, g1m=g1m, twr=tw.real, twi=tw.imag)


def _filter_positions(L):
    N = 2 * L
    i = np.arange(N)
    pos = np.where(i < L, i, N - i).astype(np.float32)
    t = pos / np.float32(max(L - 1, 1))
    w = np.float32(2.0 * math.pi) * pos / np.float32(L)
    bands = np.linspace(1e-4, HY_BANDS - 1, HY_BANDS, dtype=np.float32)
    wb = (w[:, None] * bands).astype(np.float32)
    feats = np.concatenate([t[:, None], np.cos(wb), -np.sin(wb)], axis=-1).astype(np.float32)
    fpad = np.zeros((LANES, N), np.float32)
    fpad[:HY_POS_FEAT] = feats.T
    keep = np.ones((N,), np.float32)
    keep[L] = 0.0
    n1 = N // LANES
    return fpad, t.reshape(n1, 1, LANES), keep.reshape(n1, 1, LANES)


def _filt_kernel(feat_ref, w1t_ref, b1_ref, w2t_ref, b2_ref, w3t_ref, sf_ref, ld_ref, t_ref, keep_ref,
                 ts_ref, scale_ref, ss_scr, dec_scr, *, inv_n, half_steps):
    i = pl.program_id(0)

    @pl.when(i % half_steps == 0)
    def _():
        dec_scr[...] = jnp.exp(ld_ref[0])

    @pl.when(i == 0)
    def _():
        ss_scr[...] = jnp.zeros_like(ss_scr)

    h1 = jnp.sin(sf_ref[:, 0:1] * (_dot(w1t_ref[...], feat_ref[...], precision=HIGHEST) + b1_ref[...]))
    h2 = jnp.sin(sf_ref[:, 1:2] * (_dot(w2t_ref[...], h1, precision=HIGHEST) + b2_ref[...]))
    w_hi, w_lo = _split_bf16(w3t_ref[0])
    h_hi, h_lo = _split_bf16(h2)
    ss = ss_scr[...]
    for k in range(FILT_SEG):
        sl = slice(k * LANES, (k + 1) * LANES)
        h3t = _dot(w_hi, h_hi[:, sl]) + _dot(w_hi, h_lo[:, sl]) + _dot(w_lo, h_hi[:, sl])
        win = jnp.exp(-dec_scr[...] * t_ref[k])
        ts = h3t * win * keep_ref[k]
        ts_ref[k] = ts.astype(BF16)
        ss = ss + ts * ts
    ss_scr[...] = ss

    @pl.when(i == pl.num_programs(0) - 1)
    def _():
        tot = jnp.sum(ss_scr[...], axis=-1, keepdims=True)
        scale_ref[...] = lax.rsqrt(tot + EPS) * inv_n


def _filt_fft_kernel(ts_ref, ffm_ref, twr_ref, twi_ref, f2m_ref, hr_ref, hi_ref, *, n1, cb):
    tr, ti = twr_ref[...], twi_ref[...]
    group = 8
    for g in range(cb // group):
        a = _dot(ffm_ref[...], ts_ref[:, g * group * LANES:(g + 1) * group * LANES])
        for c in range(group):
            sl = slice(c * LANES, (c + 1) * LANES)
            ar, ai = a[:n1, sl], a[n1:, sl]
            cat = jnp.concatenate([ar * tr - ai * ti, ar * ti + ai * tr], axis=1).astype(BF16)
            x = _dot(cat, f2m_ref[...]).astype(BF16)
            out = slice((g * group + c) * LANES, (g * group + c + 1) * LANES)
            hr_ref[:, out] = x[:, :LANES]
            hi_ref[:, out] = x[:, LANES:]


def _hyena_filters(L, w1, b1, w2, b2, w3, sin_freq, log_decay, cb=32):
    tabs = _dft_tables(L)
    n1 = tabs["n1"]
    N = 2 * L
    nch = HY_ORDER * B_WIDTH
    fpad, tpos, keep = _filter_positions(L)
    w1t = jnp.zeros((HY_FILT_HID, LANES), F32).at[:, :HY_POS_FEAT].set(w1.astype(F32).T)
    w3t = w3.astype(F32).T.reshape(2, nch, HY_FILT_HID)
    ld = log_decay.astype(F32).reshape(2, nch, 1)
    half = n1 // 2 // FILT_SEG
    col = lambda v: v.astype(F32).reshape(-1, 1)
    fixed = lambda shape: pl.BlockSpec(shape, lambda i: (0,) * len(shape))
    ts, scale = pl.pallas_call(
        functools.partial(_filt_kernel, inv_n=1.0 / N, half_steps=half),
        grid=(n1 // FILT_SEG,),
        in_specs=[pl.BlockSpec((LANES, FILT_SEG * LANES), lambda i: (0, i)),
                  fixed((HY_FILT_HID, LANES)), fixed((HY_FILT_HID, 1)),
                  fixed((HY_FILT_HID, HY_FILT_HID)), fixed((HY_FILT_HID, 1)),
                  pl.BlockSpec((1, nch, HY_FILT_HID), lambda i: (i // half, 0, 0)),
                  fixed((HY_FILT_HID, 2)),
                  pl.BlockSpec((1, nch, 1), lambda i: (i // half, 0, 0)),
                  pl.BlockSpec((FILT_SEG, 1, LANES), lambda i: (i, 0, 0)),
                  pl.BlockSpec((FILT_SEG, 1, LANES), lambda i: (i, 0, 0))],
        out_specs=[pl.BlockSpec((FILT_SEG, nch, LANES), lambda i: (i, 0, 0)),
                   pl.BlockSpec((nch, 1), lambda i: (0, 0))],
        out_shape=[jax.ShapeDtypeStruct((n1, nch, LANES), BF16),
                   jax.ShapeDtypeStruct((nch, 1), F32)],
        scratch_shapes=[pltpu.VMEM((nch, LANES), F32), pltpu.VMEM((nch, 1), F32)],
        compiler_params=_params("arbitrary"),
        name="hy_filter",
    )(jnp.asarray(fpad), w1t, col(b1), w2.astype(F32).T, col(b2), w3t, sin_freq.astype(F32).T, ld,
      jnp.asarray(tpos), jnp.asarray(keep))
    W = cb * LANES
    hr, hi = pl.pallas_call(
        functools.partial(_filt_fft_kernel, n1=n1, cb=cb),
        grid=(nch // cb,),
        in_specs=[pl.BlockSpec((n1, W), lambda j: (0, j)),
                  pl.BlockSpec((2 * n1, n1), lambda j: (0, 0)),
                  pl.BlockSpec((n1, LANES), lambda j: (0, 0)),
                  pl.BlockSpec((n1, LANES), lambda j: (0, 0)),
                  pl.BlockSpec((2 * LANES, 2 * LANES), lambda j: (0, 0))],
        out_specs=[pl.BlockSpec((n1, W), lambda j: (0, j)),
                   pl.BlockSpec((n1, W), lambda j: (0, j))],
        out_shape=[jax.ShapeDtypeStruct((n1, nch * LANES), BF16),
                   jax.ShapeDtypeStruct((n1, nch * LANES), BF16)],
        compiler_params=_params("arbitrary"),
        name="hy_filter_fft",
    )(ts.reshape(n1, nch * LANES), jnp.asarray(tabs["ffm"], F32).astype(BF16), jnp.asarray(tabs["twr"], F32),
      jnp.asarray(tabs["twi"], F32), jnp.asarray(tabs["f2m"], F32).astype(BF16))
    scale_rep = jnp.repeat(scale.reshape(HY_ORDER, B_WIDTH), LANES, axis=1)
    return hr, hi, scale_rep


def _shift_time(x, direction):
    rows = x.shape[0]
    lane = lax.broadcasted_iota(jnp.int32, x.shape, 1)
    row = lax.broadcasted_iota(jnp.int32, x.shape, 0)
    top = row[:SUBLANES]
    if direction > 0:
        r = pltpu.roll(x, 1, axis=1)
        e = pltpu.roll(r, 1, axis=0)
        edge = jnp.concatenate([jnp.where(top == 0, 0.0, e[:SUBLANES]), e[SUBLANES:]], axis=0)
        return jnp.where(lane == 0, edge, r)
    r = pltpu.roll(x, LANES - 1, axis=1)
    e = pltpu.roll(r, rows - 1, axis=0)
    edge = jnp.concatenate([e[:rows - SUBLANES], jnp.where(top == SUBLANES - 1, 0.0, e[rows - SUBLANES:])], axis=0)
    return jnp.where(lane == LANES - 1, edge, r)


def _hyena_kernel(g0_ref, g1_ref, z_ref, bg_ref, cw0_ref, cw1_ref, cwz_ref, skip_ref, scale_ref,
                  h0r_ref, h0i_ref, h1r_ref, h1i_ref, f1m_ref, g1m_ref, f2m_ref, g2m_ref, twr_ref, twi_ref,
                  o_ref, z_scr, g0_scr, g1_scr, a_scr, b_scr, *, n1, cb, bsz):
    nh = n1 // 2
    npair = bsz // 2
    tile = lambda i: slice(i * LANES, (i + 1) * LANES)

    def short_conv(src, cw_ref, dst):
        for c in range(cb):
            w0, w1, w2, bias = (cw_ref[k:k + 1, tile(c)] for k in range(4))
            for b in range(bsz):
                x = src[:, tile(c * bsz + b)].astype(F32)
                y = bias + w0 * _shift_time(x, 1) + w1 * x + w2 * _shift_time(x, -1)
                half, p = divmod(b, npair)
                dst[half * nh:(half + 1) * nh, tile(c * npair + p)] = y

    short_conv(g0_ref, cw0_ref, g0_scr)
    short_conv(g1_ref, cw1_ref, g1_scr)
    short_conv(z_ref, cwz_ref, z_scr)
    tr, ti = twr_ref[...].astype(BF16), twi_ref[...].astype(BF16)

    def long_conv(hr_ref, hi_ref):
        a_scr[...] = _dot(f1m_ref[...], z_scr[...].astype(BF16)).astype(BF16)
        for c in range(cb):
            hr, hi = hr_ref[:, tile(c)], hi_ref[:, tile(c)]
            for p in range(npair):
                sl = tile(c * npair + p)
                ar, ai = a_scr[0:n1, sl], a_scr[n1:2 * n1, sl]
                x = _dot(jnp.concatenate([ar * tr - ai * ti, ar * ti + ai * tr], axis=1), f2m_ref[...]).astype(BF16)
                xr, xi = x[:, :LANES], x[:, LANES:]
                y = _dot(jnp.concatenate([xr * hr - xi * hi, xr * hi + xi * hr], axis=1), g2m_ref[...]).astype(BF16)
                br, bi = y[:, :LANES], y[:, LANES:]
                b_scr[0:n1, sl] = br * tr + bi * ti
                b_scr[n1:2 * n1, sl] = bi * tr - br * ti
        return _dot(g1m_ref[...], b_scr[...])

    def per_channel(ref, row):
        return jnp.concatenate([ref[row:row + 1, tile(c)] for c in range(cb) for _ in range(npair)], axis=1)

    z0 = z_scr[...]
    z1 = g0_scr[...] * (long_conv(h0r_ref, h0i_ref) * per_channel(scale_ref, 0) + z0 * per_channel(skip_ref, 0))
    z_scr[...] = z1
    z2 = g1_scr[...] * (long_conv(h1r_ref, h1i_ref) * per_channel(scale_ref, 1) + z1 * per_channel(skip_ref, 1))
    z_scr[...] = z2
    for c in range(cb):
        for b in range(bsz):
            half, p = divmod(b, npair)
            gate = _silu(bg_ref[:, tile(c * bsz + b)].astype(F32))
            o_ref[:, tile(c * bsz + b)] = (z_scr[half * nh:(half + 1) * nh, tile(c * npair + p)] * gate).astype(BF16)


def _hyena(pt2, filt, conv_w, conv_b, skip, L, bsz, cb=8):
    hr, hi, scale_rep = filt
    tabs = _dft_tables(L)
    n1 = tabs["n1"]
    nh = n1 // 2
    assert nh % SUBLANES == 0, "sequence length must be a multiple of 1024"
    wd = cb * bsz * LANES
    wc = cb * LANES
    ws = cb * (bsz // 2) * LANES
    ncb = B_WIDTH // cb
    cw = jnp.repeat(jnp.concatenate([conv_w.astype(F32), conv_b.astype(F32)[None]], axis=0), LANES, axis=1)
    sk = jnp.repeat(skip.astype(F32), LANES, axis=1)
    data = lambda row0: pl.BlockSpec((nh, wd), lambda j: (0, row0 // cb + j))
    cws = lambda grp: pl.BlockSpec((4, wc), lambda j: (0, grp * ncb + j))
    hs = lambda order: pl.BlockSpec((n1, wc), lambda j: (0, order * ncb + j))
    const = lambda j: (0, 0)
    return pl.pallas_call(
        functools.partial(_hyena_kernel, n1=n1, cb=cb, bsz=bsz),
        grid=(ncb,),
        in_specs=[data(CT_BU), data(CT_BU + B_WIDTH), data(CT_BU + 2 * B_WIDTH), data(CT_BG),
                  cws(0), cws(1), cws(2),
                  pl.BlockSpec((2, wc), lambda j: (0, j)), pl.BlockSpec((2, wc), lambda j: (0, j)),
                  hs(0), hs(0), hs(1), hs(1),
                  pl.BlockSpec((2 * n1, n1), const), pl.BlockSpec((n1, 2 * n1), const),
                  pl.BlockSpec((2 * LANES, 2 * LANES), const), pl.BlockSpec((2 * LANES, 2 * LANES), const),
                  pl.BlockSpec((n1, LANES), const), pl.BlockSpec((n1, LANES), const)],
        out_specs=pl.BlockSpec((nh, wd), lambda j: (0, j)),
        out_shape=jax.ShapeDtypeStruct((nh, B_WIDTH * bsz * LANES), BF16),
        scratch_shapes=[pltpu.VMEM((n1, ws), F32), pltpu.VMEM((n1, ws), F32), pltpu.VMEM((n1, ws), F32),
                        pltpu.VMEM((2 * n1, ws), BF16), pltpu.VMEM((2 * n1, ws), BF16)],
        compiler_params=_params("arbitrary"),
        name="hyena",
    )(pt2, pt2, pt2, pt2, cw, cw, cw, sk, scale_rep, hr, hi, hr, hi,
      jnp.asarray(tabs["f1m"], F32).astype(BF16), jnp.asarray(tabs["g1m"], F32).astype(BF16),
      jnp.asarray(tabs["f2m"], F32).astype(BF16), jnp.asarray(tabs["g2m"], F32).astype(BF16),
      jnp.asarray(tabs["twr"], F32), jnp.asarray(tabs["twi"], F32))


def _gla_prep(q, z, v, lb, tri, forward):
    C = GLA_CHUNK
    half, quarter = C // 2, C // 4
    one_m = 1.0 - lb
    sig = _sigmoid(z)
    f = lb + one_m * sig
    logf = jnp.log2(jnp.maximum(f, F_FLOOR))
    kk = one_m * (1.0 - sig)
    l_hi, l_lo = _split_bf16(logf)
    acc = _dot(tri, jnp.concatenate([l_hi, l_lo], axis=0))
    if forward:
        tot = acc[C - 1:C]
        a_low, a_high, a_cross = acc[quarter - 1:quarter], acc[half + quarter - 1:half + quarter], acc[half - 1:half]
    else:
        tot = acc[0:1]
        a_low, a_high, a_cross = acc[quarter:quarter + 1], acc[half + quarter:half + quarter + 1], acc[half:half + 1]
    e_leaf = jnp.concatenate([acc[:half] - a_low, acc[half:] - a_high], axis=0)
    ql = (q * jnp.exp2(e_leaf)).astype(BF16)
    kl = (kk * jnp.exp2(-e_leaf)).astype(BF16)
    if forward:
        qc = (q[half:] * jnp.exp2(acc[half:] - a_cross)).astype(BF16)
        kc = (kk[:half] * jnp.exp2(a_cross - acc[:half])).astype(BF16)
    else:
        qc = (q[:half] * jnp.exp2(acc[:half] - a_cross)).astype(BF16)
        kc = (kk[half:] * jnp.exp2(a_cross - acc[half:])).astype(BF16)
    qi = (q * jnp.exp2(acc)).astype(BF16)
    ks = (kk * jnp.exp2(tot - acc)).astype(BF16)
    return ql, kl, qc, kc, qi, ks, jnp.exp2(tot), v.astype(BF16)


def _gla_matmuls(ops, st_ref, o_ref, rows, forward, p):
    ql, kl, qc, kc, qi, ks, dec, vb = ops
    C = GLA_CHUNK
    half = C // 2
    r2 = lax.broadcasted_iota(jnp.int32, (LANES, LANES), 0) < C_HEAD_DIM
    c2 = lax.broadcasted_iota(jnp.int32, (LANES, LANES), 1) < C_HEAD_DIM
    same_head = r2 == c2

    def by_head(x):
        first = lax.broadcasted_iota(jnp.int32, x.shape, 1) % LANES < C_HEAD_DIM
        return jnp.concatenate([x * jnp.where(first, 1.0, 0.0).astype(BF16),
                                x * jnp.where(first, 0.0, 1.0).astype(BF16)], axis=0)

    zeros = jnp.zeros((half, LANES), BF16)
    ti = lax.broadcasted_iota(jnp.int32, (2 * half, C), 0) % half
    si = lax.broadcasted_iota(jnp.int32, (2 * half, C), 1)
    if forward:
        own_lo = _dot_nt(by_head(ql[:half]), jnp.concatenate([kl[:half], zeros], axis=0))
        mix_hi = _dot_nt(by_head(jnp.concatenate([ql[half:], qc], axis=1)),
                         jnp.concatenate([jnp.concatenate([zeros, kl[half:]], axis=0),
                                          jnp.concatenate([kc, zeros], axis=0)], axis=1))
        s_lo = jnp.where(ti >= si, own_lo, 0.0)
        s_hi = jnp.where(ti + half >= si, mix_hi, 0.0)
    else:
        own_hi = _dot_nt(by_head(ql[half:]), jnp.concatenate([zeros, kl[half:]], axis=0))
        mix_lo = _dot_nt(by_head(jnp.concatenate([ql[:half], qc], axis=1)),
                         jnp.concatenate([jnp.concatenate([kl[:half], zeros], axis=0),
                                          jnp.concatenate([zeros, kc], axis=0)], axis=1))
        s_lo = jnp.where(si >= ti, mix_lo, 0.0)
        s_hi = jnp.where(si >= ti + half, own_hi, 0.0)
    s = jnp.concatenate([jnp.concatenate([s_lo[:half], s_hi[:half]], axis=0),
                         jnp.concatenate([s_lo[half:], s_hi[half:]], axis=0)], axis=1).astype(BF16)
    intra = _dot(s, by_head(vb))
    st = st_ref[p]
    inter = _dot_nt(qi, st.astype(BF16))
    o_ref[0, rows, p * LANES:(p + 1) * LANES] = (intra + inter).astype(o_ref.dtype)
    upd = _dot_tn(vb, ks)
    st_ref[p] = st * dec + jnp.where(same_head, upd, 0.0)


def _hgrn_kernel(qf_ref, zf_ref, vf_ref, qb_ref, zb_ref, vb_ref, lb_ref, tri_ref, of_ref, ob_ref, stf, stb, *, nsub):
    @pl.when(pl.program_id(1) == 0)
    def _():
        stf[...] = jnp.zeros_like(stf)
        stb[...] = jnp.zeros_like(stb)

    items = []
    for j in range(nsub):
        for p in range(C_WIDTH // LANES):
            items.append((qf_ref, zf_ref, vf_ref, of_ref, stf, 0, slice(j * GLA_CHUNK, (j + 1) * GLA_CHUNK), True, p))
            items.append((qb_ref, zb_ref, vb_ref, ob_ref, stb, 1,
                          slice((nsub - 1 - j) * GLA_CHUNK, (nsub - j) * GLA_CHUNK), False, p))

    def prep(item):
        q_ref, z_ref, v_ref, _, _, d, rows, forward, p = item
        sl = slice(p * LANES, (p + 1) * LANES)
        return _gla_prep(q_ref[0, rows, sl].astype(F32), z_ref[0, rows, sl].astype(F32),
                         v_ref[0, rows, sl].astype(F32), lb_ref[d:d + 1, sl], tri_ref[d], forward)

    ops = prep(items[0])
    for i, item in enumerate(items):
        nxt = prep(items[i + 1]) if i + 1 < len(items) else None
        _gla_matmuls(ops, item[4], item[3], item[6], item[7], item[8])
        ops = nxt


def _hgrn(c_all, lb, rb):
    bsz, L, _ = c_all.shape
    nblk = L // rb
    w = C_WIDTH
    fwd = lambda col: pl.BlockSpec((1, rb, w), lambda b, i: (b, i, col))
    bwd = lambda col: pl.BlockSpec((1, rb, w), lambda b, i: (b, nblk - 1 - i, col))
    npair = w // LANES
    idx = np.arange(GLA_CHUNK)
    tri = np.stack([idx[:, None] >= idx[None, :], idx[None, :] >= idx[:, None]])
    tri = jnp.asarray(np.concatenate([tri, tri], axis=2), BF16)
    return pl.pallas_call(
        functools.partial(_hgrn_kernel, nsub=rb // GLA_CHUNK),
        grid=(bsz, nblk),
        in_specs=[fwd(0), fwd(1), fwd(3), bwd(0), bwd(2), bwd(3), pl.BlockSpec((2, w), lambda b, i: (0, 0)),
                  pl.BlockSpec((2, GLA_CHUNK, 2 * GLA_CHUNK), lambda b, i: (0, 0, 0))],
        out_specs=[pl.BlockSpec((1, rb, w), lambda b, i: (b, i, 0)),
                   pl.BlockSpec((1, rb, w), lambda b, i: (b, nblk - 1 - i, 0))],
        out_shape=[jax.ShapeDtypeStruct((bsz, L, w), BF16), jax.ShapeDtypeStruct((bsz, L, w), BF16)],
        scratch_shapes=[pltpu.VMEM((npair, LANES, LANES), F32), pltpu.VMEM((npair, LANES, LANES), F32)],
        compiler_params=_params("arbitrary", "arbitrary"),
        name="hgrn",
    )(c_all, c_all, c_all, c_all, c_all, c_all, lb, tri)


def _gelu_tanh(x):
    return 0.5 * x * (1.0 + jnp.tanh(math.sqrt(2.0 / math.pi) * (x + 0.044715 * (x * x * x))))


def _out_kernel(x_ref, mod_ref, ys_ref, ag_ref, hy_ref, of_ref, ob_ref, cg_ref, gw_ref, gb_ref,
                hn_ref, ones_ref, wo_ref, fn_ref, o_ref, *, final):
    bsz = x_ref.shape[0]
    nb = bsz // 2
    m = nb * LANES
    for half in range(2):
        bs = slice(half * nb, (half + 1) * nb)
        y_rows, g_rows, b_rows = [], [], []
        for b in range(half * nb, (half + 1) * nb):
            lanes = slice(b * LANES, (b + 1) * LANES)
            y_rows.append(ys_ref[0, :, b, :].T)
            g_rows.append(ag_ref[0, :, lanes].astype(F32).T)
            b_rows.append(hy_ref[0, :, lanes].astype(F32).T)
        z = _gelu_tanh(jnp.concatenate(y_rows, axis=0))
        glu = _sigmoid(_dot(z.astype(BF16), gw_ref[...]) + gb_ref[...])
        a_out = (z * glu * _silu(jnp.concatenate(g_rows, axis=0))).astype(BF16)
        b_out = jnp.concatenate(b_rows, axis=0).astype(BF16)

        o = (of_ref[bs].astype(F32) + ob_ref[bs].astype(F32)).reshape(m, C_WIDTH)
        ms = _dot((o * o).astype(BF16), ones_ref[...]) * (1.0 / C_HEAD_DIM)
        gate_c = _silu(cg_ref[bs].astype(F32)).reshape(m, C_WIDTH)
        c_out = (o * lax.rsqrt(ms + EPS) * hn_ref[...] * gate_c).astype(BF16)

        mixed = _dot(jnp.concatenate([a_out, b_out, c_out], axis=1), wo_ref[...])
        xn = x_ref[bs] + mod_ref[bs, 2:3, :] * mixed.reshape(nb, LANES, D_MODEL)
        if final:
            msq = jnp.mean(xn * xn, axis=-1, keepdims=True)
            xn = xn * lax.rsqrt(msq + EPS) * fn_ref[...]
        o_ref[bs] = xn


def _out(x, mod, ys4, pt3, hy3, of, ob, c_all, gw, gb, hn, ones_bd, wo, fn, final):
    bsz, L, dm = x.shape
    row = lambda w, col: pl.BlockSpec((bsz, LANES, w), lambda i: (0, i, col))
    full = lambda a: _resident(a.shape, lambda i: (0,) * a.ndim)
    return pl.pallas_call(
        functools.partial(_out_kernel, final=final),
        grid=(L // LANES,),
        in_specs=[row(dm, 0), full(mod),
                  pl.BlockSpec((1, A_WIDTH, bsz, LANES), lambda i: (i, 0, 0, 0)),
                  pl.BlockSpec((1, A_WIDTH, bsz * LANES), lambda i: (i, CT_AG // A_WIDTH, 0)),
                  pl.BlockSpec((1, B_WIDTH, bsz * LANES), lambda i: (i, 0, 0)),
                  row(C_WIDTH, 0), row(C_WIDTH, 0), row(C_WIDTH, 4),
                  full(gw), full(gb), full(hn), full(ones_bd), full(wo), full(fn)],
        out_specs=row(dm, 0),
        out_shape=jax.ShapeDtypeStruct((bsz, L, dm), F32),
        compiler_params=_params("arbitrary"),
        name="out",
    )(x, mod, ys4, pt3, hy3, of, ob, c_all, gw, gb, hn, ones_bd, wo, fn)


def _layer(xs, mods, p, layer, hg_rb=2048):
    head = np.arange(C_WIDTH) // C_HEAD_DIM
    ones_bd = jnp.asarray(head[:, None] == head[None, :], BF16)
    w_in = p["w_in"][layer]
    o_b = 2 * A_WIDTH + 4 * B_WIDTH
    wt = w_in[:, :o_b].T.astype(BF16)
    wc = w_in[:, o_b:].astype(BF16)
    norm_w = p["norm_w"][layer].reshape(1, -1).astype(F32)
    projs = [_proj(x, mod, norm_w, wt, wc) for x, mod in zip(xs, mods)]
    bsz = xs[0].shape[0]
    d4 = jnp.broadcast_to(p["s5_d"][layer].astype(F32)[:, None, None], (A_WIDTH, bsz, LANES))
    ys4s = _s5([au for au, _, _ in projs], p["s5_consts"][layer], d4)
    out = []
    for x, mod, (au, pt, c_all), ys4 in zip(xs, mods, projs, ys4s):
        L = x.shape[1]
        nchunk = L // LANES
        hy2 = _hyena(pt.reshape(nchunk, CT_ROWS * bsz * LANES), p["hy_filters"][(layer, L)], p["hy_conv_w"][layer],
                     p["hy_conv_b"][layer], p["hy_skip"][layer], L, bsz)
        of, ob = _hgrn(c_all, p["lower_bounds"][layer], hg_rb)
        out.append(_out(x, mod, ys4, pt, hy2.reshape(nchunk, B_WIDTH, bsz * LANES), of, ob, c_all,
                        p["s5_glu_w"][layer].astype(BF16), p["s5_glu_b"][layer].reshape(1, -1).astype(F32),
                        jnp.tile(p["hg_norm_w"][layer].astype(F32), C_WIDTH // C_HEAD_DIM).reshape(1, -1),
                        ones_bd, p["w_out"][layer].astype(BF16), p["final_norm_w"].reshape(1, -1).astype(F32),
                        layer == DEPTH - 1))
    return out


def kernel(x_prompt, x_sample, c_prompt, c_sample, norm_w, ada_w, ada_b, w_in, w_out, s5_lambda_re, s5_lambda_im, s5_log_dt, s5_b_re, s5_b_im, s5_c_re, s5_c_im, s5_d, s5_glu_w, s5_glu_b, hy_conv_w, hy_conv_b, hy_w1, hy_b1, hy_w2, hy_b2, hy_w3, hy_sin_freq, hy_log_decay, hy_skip, hg_lb_logits, hg_norm_w, final_norm_w):
    bp = x_prompt.shape[0]
    ada = _ada(jnp.concatenate([c_prompt, c_sample], axis=0).astype(F32), ada_w.astype(F32), ada_b.astype(F32))
    lb_soft = jax.nn.softmax(hg_lb_logits.astype(F32), axis=0)
    lower_bounds = jnp.cumsum(lb_soft, axis=0) - lb_soft[0:1]
    lengths = sorted({x_prompt.shape[1], x_sample.shape[1]})
    p = dict(
        norm_w=norm_w, w_in=w_in, w_out=w_out, s5_d=s5_d, s5_glu_w=s5_glu_w, s5_glu_b=s5_glu_b,
        hy_conv_w=hy_conv_w, hy_conv_b=hy_conv_b, hy_skip=hy_skip, hg_norm_w=hg_norm_w, final_norm_w=final_norm_w,
        lower_bounds=lower_bounds,
        s5_consts=[_s5_consts(s5_lambda_re[l], s5_lambda_im[l], s5_log_dt[l], s5_b_re[l], s5_b_im[l],
                              s5_c_re[l], s5_c_im[l]) for l in range(DEPTH)],
        hy_filters={(l, L): _hyena_filters(L, hy_w1[l], hy_b1[l], hy_w2[l], hy_b2[l], hy_w3[l], hy_sin_freq[l],
                                           hy_log_decay[l])
                    for l in range(DEPTH) for L in lengths},
    )
    xs = [x_prompt, x_sample]
    for layer in range(DEPTH):
        mods = [ada[layer, :bp].reshape(bp, 3, D_MODEL),
                ada[layer, bp:].reshape(x_sample.shape[0], 3, D_MODEL)]
        xs = _layer(xs, mods, p, layer)
    return (xs[0], xs[1])
```
